```python
import jax, jax.numpy as jnp
from jax import lax
import numpy as np

D_MODEL = 1024
BATCH = 8
SEQ = 4096
DEPTH = 1

GRID_W = 64
CTX_LEN = 256
D_HGRN = D_MODEL // 2
HGRN_DIM = 128
HGRN_HEADS = D_HGRN // HGRN_DIM
D_FOURIER = D_MODEL - D_HGRN
FOURIER_DIM = 128
FOURIER_GROUPS = D_FOURIER // FOURIER_DIM
D_MIX = D_HGRN + D_FOURIER
N_HGRN_PROJ = 5
D_IN = N_HGRN_PROJ * D_HGRN + D_FOURIER
D_FF = ((8 * D_MODEL // 3 + 127) // 128) * 128
CHUNK = 64
EPS = 1e-6

kernel_name = "hybrid_hgrn2_fnet_convffn_dit_block"


def rms_norm(x, g):
    xf = x.astype(jnp.float32)
    y = xf * lax.rsqrt(jnp.mean(xf * xf, axis=-1, keepdims=True) + EPS)
    return (y * g.astype(jnp.float32)).astype(x.dtype)


def modulate(h, shift, scale):
    return h * (1.0 + scale) + shift


def forget_gate(logits, lb):
    f = lb + (1.0 - lb) * jax.nn.sigmoid(logits)
    return 1.0 - f, jnp.log(f)


def gla_chunked(q, k, v, logf, s0):
    B, L, H, DK = q.shape
    n = L // CHUNK
    r = lambda t: t.reshape(B, n, CHUNK, H, t.shape[-1])
    q, k, v, logf = r(q), r(k), r(v), r(logf)
    b = jnp.cumsum(logf, axis=2)
    b_ref = b[:, :, CHUNK // 2:CHUNK // 2 + 1]
    b_last = b[:, :, -1:]
    q_t = q * jnp.exp(b - b_ref)
    k_t = k * jnp.exp(b_ref - b)
    scores = jnp.einsum('bnthk,bnshk->bnhts', q_t, k_t)
    causal_in_scan = jnp.tril(jnp.ones((CHUNK, CHUNK), dtype=bool))
    scores = jnp.where(causal_in_scan, scores, 0.0)
    o_intra = jnp.einsum('bnhts,bnshv->bnthv', scores, v)
    q_inter = q * jnp.exp(b)
    u = jnp.einsum('bnshk,bnshv->nbhkv', k * jnp.exp(b_last - b), v)
    g = jnp.transpose(jnp.exp(b_last[:, :, 0]), (1, 0, 2, 3))[..., None]

    def step(s, inp):
        g_n, u_n = inp
        return g_n * s + u_n, s

    s_final, s_prev = lax.scan(step, s0, (g, u))
    o_inter = jnp.einsum('bnthk,nbhkv->bnthv', q_inter, s_prev)
    o = (o_intra + o_inter).reshape(B, L, H, v.shape[-1])
    return o, s_final


def token_mixer(h, w_in, lb_f, lb_b, g_norm, s0_f, s0_b):
    B, L, _ = h.shape
    z = h @ w_in
    zq, zf_f, zf_b, zi, zg, zu = jnp.split(z, [D_HGRN * j for j in range(1, N_HGRN_PROJ + 1)], axis=-1)
    heads = lambda t: t.astype(jnp.float32).reshape(B, L, HGRN_HEADS, HGRN_DIM)
    q = heads(jax.nn.silu(zq))
    v = heads(zi)
    k_f, lf_f = forget_gate(heads(zf_f), lb_f.reshape(HGRN_HEADS, HGRN_DIM))
    k_b, lf_b = forget_gate(heads(zf_b), lb_b.reshape(HGRN_HEADS, HGRN_DIM))
    flip = lambda t: jnp.flip(t, axis=1)
    o_f, s_f = gla_chunked(q, k_f, v, lf_f, s0_f)
    o_b, s_b = gla_chunked(flip(q), flip(k_b), flip(v), flip(lf_b), s0_b)
    o = rms_norm(o_f + flip(o_b), g_norm) * jax.nn.silu(heads(zg))
    o = o.reshape(B, L, D_HGRN).astype(h.dtype)
    uf = zu.astype(jnp.float32).reshape(B, L, FOURIER_GROUPS, FOURIER_DIM)
    y_four = jnp.fft.fft2(uf, axes=(1, 3), norm='ortho').real
    y_four = y_four.reshape(B, L, D_FOURIER).astype(h.dtype)
    return jnp.concatenate([o, y_four], axis=-1), s_f, s_b


def conv_ffn(h, w_up, w_dw, b_dw, w_down, rows, cols):
    B, L, _ = h.shape
    a, u = jnp.split(h @ w_up, 2, axis=-1)
    a = lax.conv_general_dilated(
        a.reshape(B, rows, cols, D_FF), w_dw[:, :, None, :].astype(a.dtype),
        window_strides=(1, 1), padding=((1, 1), (1, 1)),
        dimension_numbers=('NHWC', 'HWIO', 'NHWC'), feature_group_count=D_FF)
    a = a.reshape(B, L, D_FF) + b_dw
    return (jax.nn.gelu(a) * u) @ w_down


def setup_inputs(seed: int = 0) -> dict:
    key = jax.random.key(seed)
    ks = jax.random.split(key, 20)
    nrm = lambda k, shape, s: jax.random.normal(k, shape, jnp.float32) * s
    return {
        "x": nrm(ks[0], (BATCH, SEQ, D_MODEL), 1.0),
        "c": nrm(ks[1], (BATCH, D_MODEL), 1.0),
        "ctx": nrm(ks[2], (BATCH, CTX_LEN, D_MODEL), 1.0),
        "c_ctx": nrm(ks[3], (D_MODEL,), 1.0),
        "w_mod": nrm(ks[4], (DEPTH, D_MODEL, 6 * D_MODEL), 0.5 * D_MODEL ** -0.5),
        "b_mod": nrm(ks[5], (DEPTH, 6 * D_MODEL), 0.01),
        "norm1": 1.0 + nrm(ks[6], (DEPTH, D_MODEL), 0.01),
        "norm2": 1.0 + nrm(ks[7], (DEPTH, D_MODEL), 0.01),
        "w_in": nrm(ks[8], (DEPTH, D_MODEL, D_IN), D_MODEL ** -0.5),
        "lb_fwd": nrm(ks[9], (DEPTH + 1, D_HGRN), 0.5),
        "lb_bwd": nrm(ks[10], (DEPTH + 1, D_HGRN), 0.5),
        "hgrn_norm": 1.0 + nrm(ks[11], (DEPTH, HGRN_DIM), 0.01),
        "w_out": nrm(ks[12], (DEPTH, D_MIX, D_MODEL), D_MIX ** -0.5),
        "w_up": nrm(ks[13], (DEPTH, D_MODEL, 2 * D_FF), D_MODEL ** -0.5),
        "w_dw": nrm(ks[14], (DEPTH, 3, 3, D_FF), 1.0 / 3.0),
        "b_dw": nrm(ks[15], (DEPTH, D_FF), 0.01),
        "w_down": nrm(ks[16], (DEPTH, D_FF, D_MODEL), D_FF ** -0.5),
        "norm_f": 1.0 + nrm(ks[17], (D_MODEL,), 0.01),
    }


def reference(x, c, ctx, c_ctx, w_mod, b_mod, norm1, norm2, w_in, lb_fwd, lb_bwd,
              hgrn_norm, w_out, w_up, w_dw, b_dw, w_down, norm_f):
    B, L, D = x.shape
    rows = L // GRID_W
    n_ctx = ctx.shape[1]
    lb_tab_f = jnp.cumsum(jax.nn.softmax(lb_fwd.astype(jnp.float32), axis=0), axis=0)
    lb_tab_b = jnp.cumsum(jax.nn.softmax(lb_bwd.astype(jnp.float32), axis=0), axis=0)
    zero_state = jnp.zeros((B, HGRN_HEADS, HGRN_DIM, HGRN_DIM), jnp.float32)
    for l in range(DEPTH):
        mod = (jax.nn.silu(c) @ w_mod[l] + b_mod[l])[:, None, :]
        sh1, sc1, gt1, sh2, sc2, gt2 = jnp.split(mod, 6, axis=-1)
        mod_c = jax.nn.silu(c_ctx) @ w_mod[l] + b_mod[l]
        sh1c, sc1c, gt1c, sh2c, sc2c, gt2c = jnp.split(mod_c, 6, axis=-1)
        hc = modulate(rms_norm(ctx, norm1[l]), sh1c, sc1c)
        yc, sc_f, sc_b = token_mixer(hc, w_in[l], lb_tab_f[l], lb_tab_b[l], hgrn_norm[l],
                                     zero_state, zero_state)
        h = modulate(rms_norm(x, norm1[l]), sh1, sc1)
        y, _, _ = token_mixer(h, w_in[l], lb_tab_f[l], lb_tab_b[l], hgrn_norm[l], sc_f, sc_b)
        x = x + gt1 * (y @ w_out[l])
        h2 = modulate(rms_norm(x, norm2[l]), sh2, sc2)
        x = x + gt2 * conv_ffn(h2, w_up[l], w_dw[l], b_dw[l], w_down[l], rows, GRID_W)
        if l < DEPTH - 1:
            ctx = ctx + gt1c * (yc @ w_out[l])
            hc2 = modulate(rms_norm(ctx, norm2[l]), sh2c, sc2c)
            ctx = ctx + gt2c * conv_ffn(hc2, w_up[l], w_dw[l], b_dw[l], w_down[l], 1, n_ctx)
    return rms_norm(x, norm_f)
```

```python
import functools
import math

import jax
import jax.numpy as jnp
import numpy as np
from jax import lax
from jax.experimental import pallas as pl
from jax.experimental.pallas import tpu as pltpu

EPS = 1e-6
GRID_W = 64
CHUNK = 64
HEAD = 128
N_HEADS = 4
D_HGRN = N_HEADS * HEAD
FOUR_DIM = 128
N_FOUR = 4
DFT_N2 = 64
VMEM_LIMIT = 56 * 1024 * 1024

BF16 = jnp.bfloat16
F32 = jnp.float32


def _params(*sem):
    return pltpu.CompilerParams(dimension_semantics=sem, vmem_limit_bytes=VMEM_LIMIT)


def _silu(z):
    return z * (1.0 / (1.0 + jnp.exp(-z)))


def _rms(xf, g):
    ms = jnp.mean(xf * xf, axis=-1, keepdims=True)
    return xf * lax.rsqrt(ms + EPS) * g


def _mod_kernel(c_ref, w_ref, b_ref, o_ref):
    s = _silu(c_ref[...])
    o_ref[...] = jnp.dot(s, w_ref[...], preferred_element_type=F32,
                         precision=lax.Precision.HIGHEST) + b_ref[...]


def _modulation(cc, w_mod, b_mod):
    n, d = cc.shape
    n_out = w_mod.shape[1]
    tn = 1024
    return pl.pallas_call(
        _mod_kernel,
        grid=(n_out // tn,),
        in_specs=[pl.BlockSpec((n, d), lambda j: (0, 0)),
                  pl.BlockSpec((d, tn), lambda j: (0, j)),
                  pl.BlockSpec((1, tn), lambda j: (0, j))],
        out_specs=pl.BlockSpec((n, tn), lambda j: (0, j)),
        out_shape=jax.ShapeDtypeStruct((n, n_out), F32),
        compiler_params=_params("arbitrary"),
        name="mod",
    )(cc, w_mod, b_mod)


def _lb_kernel(layer, f_ref, b_ref, o_ref):
    def table(ref):
        t = ref[...]
        e = jnp.exp(t - jnp.max(t, axis=0, keepdims=True))
        p = e / jnp.sum(e, axis=0, keepdims=True)
        return jnp.sum(p[:layer + 1], axis=0, keepdims=True)
    o_ref[0:1, :] = table(f_ref)
    o_ref[1:2, :] = table(b_ref)


def _lower_bounds(lb_fwd, lb_bwd, layer):
    return pl.pallas_call(
        functools.partial(_lb_kernel, layer),
        out_shape=jax.ShapeDtypeStruct((2, lb_fwd.shape[1]), F32),
        name="lb",
    )(lb_fwd, lb_bwd)


_G_Q, _G_FF, _G_FB, _G_V, _G_G, _G_U = range(6)


def _inproj_kernel(groups, x_ref, sh_ref, sc_ref, g_ref, lb_ref, w_ref, *out_refs):
    xf = x_ref[0]
    h = (_rms(xf, g_ref[...]) * (1.0 + sc_ref[0]) + sh_ref[0]).astype(BF16)
    outs = iter(out_refs)
    for grp in groups:
        z = jnp.dot(h, w_ref[:, grp * D_HGRN:(grp + 1) * D_HGRN], preferred_element_type=F32)
        if grp in (_G_Q, _G_G):
            o = next(outs)
            o[0] = _silu(z).astype(o.dtype)
        elif grp in (_G_FF, _G_FB):
            lb = lb_ref[0:1, :] if grp == _G_FF else lb_ref[1:2, :]
            f = lb + (1.0 - lb) * (1.0 / (1.0 + jnp.exp(-z)))
            ko, lo = next(outs), next(outs)
            ko[0] = (1.0 - f).astype(ko.dtype)
            lo[0] = jnp.log(f)
        else:
            o = next(outs)
            o[0] = z.astype(o.dtype)


def _inproj(x, sh, sc, g, lb, w_in, groups, tm):
    B, L, D = x.shape
    tm = min(tm, L)
    out_shape, out_specs = [], []
    tok = lambda b, i: (b, i, 0)
    for grp in groups:
        dts = (BF16, F32) if grp in (_G_FF, _G_FB) else (BF16,)
        for dt in dts:
            out_shape.append(jax.ShapeDtypeStruct((B, L, D_HGRN), dt))
            out_specs.append(pl.BlockSpec((1, tm, D_HGRN), tok))
    return pl.pallas_call(
        functools.partial(_inproj_kernel, groups),
        grid=(B, L // tm),
        in_specs=[pl.BlockSpec((1, tm, D), tok),
                  pl.BlockSpec((1, 1, D), lambda b, i: (b, 0, 0)),
                  pl.BlockSpec((1, 1, D), lambda b, i: (b, 0, 0)),
                  pl.BlockSpec((1, D), lambda b, i: (0, 0)),
                  pl.BlockSpec(lb.shape, lambda b, i: (0, 0)),
                  pl.BlockSpec(w_in.shape, lambda b, i: (0, 0))],
        out_specs=out_specs,
        out_shape=out_shape,
        compiler_params=_params("arbitrary", "arbitrary"),
        name="inproj",
    )(x, sh, sc, g, lb, w_in)


def _split3(a):
    hi = a.astype(BF16)
    r = a - hi.astype(F32)
    mid = r.astype(BF16)
    lo = (r - mid.astype(F32)).astype(BF16)
    return hi, mid, lo


def _gla_direction(rev, need_o, nc, q_ref, k_ref, lf_ref, v_ref, st_ref, o_ref):
    row = lax.broadcasted_iota(jnp.int32, (CHUNK, CHUNK), 0)
    col = lax.broadcasted_iota(jnp.int32, (CHUNK, CHUNK), 1)
    keep = (row <= col) if rev else (row >= col)
    tri = keep.astype(BF16)
    i_ref = CHUNK - 1 - CHUNK // 2 if rev else CHUNK // 2
    i_last = 0 if rev else CHUNK - 1
    chunks = range(nc - 1, -1, -1) if rev else range(nc)
    for c in chunks:
        rows = slice(c * CHUNK, (c + 1) * CHUNK)
        hi, mid, lo = _split3(lf_ref[0, rows, :])
        cum = (jnp.dot(tri, hi, preferred_element_type=F32)
               + jnp.dot(tri, mid, preferred_element_type=F32)
               + jnp.dot(tri, lo, preferred_element_type=F32))
        for h in range(N_HEADS):
            cols = slice(h * HEAD, (h + 1) * HEAD)
            bh = cum[:, cols]
            b_ref = bh[i_ref:i_ref + 1]
            b_last = bh[i_last:i_last + 1]
            e_fwd = jnp.exp(bh - b_ref)
            e_bwd = jnp.exp(b_ref - bh)
            kh = k_ref[0, rows, cols].astype(F32) * e_bwd
            vh = v_ref[0, rows, cols]
            st = st_ref[0, h]
            if need_o:
                qh = q_ref[0, rows, cols].astype(F32) * e_fwd
                s = lax.dot_general(qh.astype(BF16), kh.astype(BF16), (((1,), (1,)), ((), ())),
                                    preferred_element_type=F32)
                s = jnp.where(keep, s, 0.0)
                o = jnp.dot(s.astype(BF16), vh, preferred_element_type=F32)
                q_inter = (qh * jnp.exp(b_ref)).astype(BF16)
                o = o + lax.dot_general(q_inter, st.astype(BF16), (((1,), (1,)), ((), ())),
                                        preferred_element_type=F32)
                o_ref[0, rows, cols] = o.astype(o_ref.dtype)
            k_dec = (kh * jnp.exp(b_last - b_ref)).astype(BF16)
            u_t = lax.dot_general(vh, k_dec, (((0,), (0,)), ((), ())), preferred_element_type=F32)
            st_ref[0, h] = jnp.exp(b_last) * st + u_t


def _gla_kernel(need_o, nc, *refs):
    if need_o:
        (qf, kf, lff, vf, qb, kb, lfb, vb, s0f, s0b, of, ob, sf, sb) = refs
    else:
        (kf, lff, vf, kb, lfb, vb, s0f, s0b, sf, sb) = refs
        qf = qb = of = ob = None

    @pl.when(pl.program_id(1) == 0)
    def _():
        sf[...] = s0f[...]
        sb[...] = s0b[...]

    _gla_direction(False, need_o, nc, qf, kf, lff, vf, sf, of)
    _gla_direction(True, need_o, nc, qb, kb, lfb, vb, sb, ob)


def _gla(q, kf, lff, kb, lfb, v, s0f, s0b, need_o, tl):
    B, L, _ = v.shape
    tl = min(tl, L)
    nt = L // tl
    fwd = pl.BlockSpec((1, tl, D_HGRN), lambda b, i: (b, i, 0))
    bwd = pl.BlockSpec((1, tl, D_HGRN), lambda b, i: (b, nt - 1 - i, 0))
    st = pl.BlockSpec((1, N_HEADS, HEAD, HEAD), lambda b, i: (b, 0, 0, 0))
    st_shape = jax.ShapeDtypeStruct((B, N_HEADS, HEAD, HEAD), F32)
    if need_o:
        ins = (q, kf, lff, v, q, kb, lfb, v, s0f, s0b)
        in_specs = [fwd] * 4 + [bwd] * 4 + [st, st]
        out_specs = [fwd, bwd, st, st]
        o_shape = jax.ShapeDtypeStruct((B, L, D_HGRN), BF16)
        out_shape = [o_shape, o_shape, st_shape, st_shape]
    else:
        ins = (kf, lff, v, kb, lfb, v, s0f, s0b)
        in_specs = [fwd] * 3 + [bwd] * 3 + [st, st]
        out_specs = [st, st]
        out_shape = [st_shape, st_shape]
    return pl.pallas_call(
        functools.partial(_gla_kernel, need_o, tl // CHUNK),
        grid=(B, nt),
        in_specs=in_specs,
        out_specs=out_specs,
        out_shape=out_shape,
        compiler_params=_params("arbitrary", "arbitrary"),
        name="gla",
    )(*ins)


def _dft_tables(L):
    n1, n2 = L // DFT_N2, DFT_N2
    scale = 1.0 / math.sqrt(L * FOUR_DIM)
    k1 = np.arange(n1)[:, None]
    t1 = np.arange(n1)[None, :]
    ang = 2.0 * np.pi * ((k1 * t1) % n1) / n1
    fa = np.concatenate([np.cos(ang), -np.sin(ang)], axis=0)
    t2 = np.arange(n2)[:, None]
    angw = 2.0 * np.pi * ((t2 * np.arange(n1)[None, :]) % L) / L
    cw = np.broadcast_to(np.cos(angw)[:, :, None], (n2, n1, 128))
    sw = np.broadcast_to(np.sin(angw)[:, :, None], (n2, n1, 128))
    k2 = np.arange(n2)[:, None]
    ang2 = 2.0 * np.pi * ((k2 * np.arange(n2)[None, :]) % n2) / n2
    c2, s2 = np.cos(ang2), np.sin(ang2)
    fb = np.block([[c2, s2], [-s2, c2]])
    n = np.arange(FOUR_DIM)[:, None]
    angc = 2.0 * np.pi * ((n * np.arange(FOUR_DIM)[None, :]) % FOUR_DIM) / FOUR_DIM
    ch = np.concatenate([np.cos(angc), np.sin(angc)], axis=0) * scale
    f32 = lambda a: jnp.asarray(a, F32)
    return (f32(fa).astype(BF16), f32(cw), f32(sw), f32(fb).astype(BF16), f32(ch).astype(BF16))


def _dft_a_kernel(n1, n_t2, fa_ref, cw_ref, sw_ref, u_ref, o_ref):
    v = jnp.dot(fa_ref[...], u_ref[0], preferred_element_type=F32)
    for j in range(n_t2):
        cw, sw = cw_ref[j], sw_ref[j]
        for g in range(N_FOUR):
            cols = slice(j * N_FOUR * FOUR_DIM + g * FOUR_DIM, j * N_FOUR * FOUR_DIM + (g + 1) * FOUR_DIM)
            vr, vi = v[:n1, cols], v[n1:, cols]
            o_ref[0, :n1, cols] = (vr * cw + vi * sw).astype(o_ref.dtype)
            o_ref[0, n1:, cols] = (vi * cw - vr * sw).astype(o_ref.dtype)


def _dft_b_kernel(n_k1, fb_ref, ch_ref, re_ref, im_ref, o_ref):
    n2 = DFT_N2
    d = N_FOUR * FOUR_DIM
    for j in range(n_k1):
        stack = jnp.concatenate([re_ref[0, j], im_ref[0, j]], axis=0)
        x = jnp.dot(fb_ref[...], stack, preferred_element_type=F32)
        for g in range(N_FOUR):
            cols = slice(g * FOUR_DIM, (g + 1) * FOUR_DIM)
            xg = jnp.concatenate([x[:n2, cols], x[n2:, cols]], axis=1).astype(BF16)
            y = jnp.dot(xg, ch_ref[...], preferred_element_type=F32)
            o_ref[0, :, j * d + g * FOUR_DIM:j * d + (g + 1) * FOUR_DIM] = y.astype(o_ref.dtype)


def _fourier(u):
    B, L, D = u.shape
    n1, n2 = L // DFT_N2, DFT_N2
    fa, cw, sw, fb, ch = _dft_tables(L)
    n_t2 = 8
    const2 = lambda b, i: (0, 0)
    v = pl.pallas_call(
        functools.partial(_dft_a_kernel, n1, n_t2),
        grid=(B, n2 // n_t2),
        in_specs=[pl.BlockSpec(fa.shape, const2),
                  pl.BlockSpec((n_t2, n1, 128), lambda b, i: (i, 0, 0)),
                  pl.BlockSpec((n_t2, n1, 128), lambda b, i: (i, 0, 0)),
                  pl.BlockSpec((1, n1, n_t2 * D), lambda b, i: (b, 0, i))],
        out_specs=pl.BlockSpec((1, 2 * n1, n_t2 * D), lambda b, i: (b, 0, i)),
        out_shape=jax.ShapeDtypeStruct((B, 2 * n1, n2 * D), BF16),
        compiler_params=_params("arbitrary", "arbitrary"),
        name="dft_a",
    )(fa, cw, sw, u.reshape(B, n1, n2 * D))
    v = v.reshape(B, 2 * n1, n2, D)
    n_k1 = min(8, n1)
    nb = n1 // n_k1
    y = pl.pallas_call(
        functools.partial(_dft_b_kernel, n_k1),
        grid=(B, nb),
        in_specs=[pl.BlockSpec(fb.shape, const2),
                  pl.BlockSpec(ch.shape, const2),
                  pl.BlockSpec((1, n_k1, n2, D), lambda b, i: (b, i, 0, 0)),
                  pl.BlockSpec((1, n_k1, n2, D), lambda b, i: (b, nb + i, 0, 0))],
        out_specs=pl.BlockSpec((1, n2, n_k1 * D), lambda b, i: (b, 0, i)),
        out_shape=jax.ShapeDtypeStruct((B, n2, n1 * D), BF16),
        compiler_params=_params("arbitrary", "arbitrary"),
        name="dft_b",
    )(fb, ch, v, v)
    return y.reshape(B, L, D)


def _outproj_kernel(x_ref, of_ref, ob_ref, sg_ref, yf_ref, gt_ref, gn_ref, w_ref, o_ref):
    o = of_ref[0].astype(F32) + ob_ref[0].astype(F32)
    parts = []
    for h in range(N_HEADS):
        cols = slice(h * HEAD, (h + 1) * HEAD)
        parts.append(_rms(o[:, cols], gn_ref[...]))
    o = jnp.concatenate(parts, axis=1) * sg_ref[0].astype(F32)
    mix = jnp.concatenate([o.astype(BF16), yf_ref[0]], axis=1)
    y = jnp.dot(mix, w_ref[...], preferred_element_type=F32)
    o_ref[0] = x_ref[0] + gt_ref[0] * y


def _outproj(x, of, ob, sg, yf, gt1, gn, w_out, tm):
    B, L, D = x.shape
    tm = min(tm, L)
    tok = lambda b, i: (b, i, 0)
    half = pl.BlockSpec((1, tm, D_HGRN), tok)
    return pl.pallas_call(
        _outproj_kernel,
        grid=(B, L // tm),
        in_specs=[pl.BlockSpec((1, tm, D), tok), half, half, half, half,
                  pl.BlockSpec((1, 1, D), lambda b, i: (b, 0, 0)),
                  pl.BlockSpec((1, HEAD), lambda b, i: (0, 0)),
                  pl.BlockSpec(w_out.shape, lambda b, i: (0, 0))],
        out_specs=pl.BlockSpec((1, tm, D), tok),
        out_shape=jax.ShapeDtypeStruct((B, L, D), F32),
        compiler_params=_params("arbitrary", "arbitrary"),
        name="outproj",
    )(x, of, ob, sg, yf, gt1, gn, w_out)


def _gelu_tanh(a):
    return 0.5 * a * (1.0 + jnp.tanh(0.7978845608028654 * (a + 0.044715 * (a * a * a))))


def _ffn_kernel(tm, nt, x_ref, top_ref, bot_ref, sh_ref, sc_ref, gt_ref, g2_ref, gf_ref,
                wa_ref, wu_ref, wdw_ref, bdw_ref, wd_ref, o_ref, h_ref, acc_ref):
    i = pl.program_id(1)
    f = pl.program_id(2)
    W = GRID_W

    @pl.when(f == 0)
    def _():
        def prep(xf):
            return (_rms(xf, g2_ref[...]) * (1.0 + sc_ref[0]) + sh_ref[0]).astype(BF16)
        h_ref[0:W, :] = prep(top_ref[0])
        h_ref[W:W + tm, :] = prep(x_ref[0])
        h_ref[W + tm:, :] = prep(bot_ref[0])
        acc_ref[...] = jnp.zeros_like(acc_ref)

    n_rows = tm + 2 * W
    a = jnp.dot(h_ref[...], wa_ref[...], preferred_element_type=F32)
    r = lax.broadcasted_iota(jnp.int32, (n_rows, 1), 0)
    outside = ((r < W) & (i == 0)) | ((r >= W + tm) & (i == nt - 1))
    a = jnp.where(outside, 0.0, a)
    colx = r % W
    a_m = jnp.where(colx == 0, 0.0, pltpu.roll(a, 1, axis=0))
    a_p = jnp.where(colx == W - 1, 0.0, pltpu.roll(a, n_rows - 1, axis=0))
    conv = bdw_ref[...]
    for dr in range(3):
        rows = slice(dr * W, dr * W + tm)
        conv = conv + (wdw_ref[3 * dr:3 * dr + 1, :] * a_m[rows]
                       + wdw_ref[3 * dr + 1:3 * dr + 2, :] * a[rows]
                       + wdw_ref[3 * dr + 2:3 * dr + 3, :] * a_p[rows])
    u = jnp.dot(h_ref[W:W + tm, :], wu_ref[...], preferred_element_type=F32)
    gate = (_gelu_tanh(conv) * u).astype(BF16)
    acc_ref[...] += jnp.dot(gate, wd_ref[...], preferred_element_type=F32)

    @pl.when(f == pl.num_programs(2) - 1)
    def _():
        x2 = x_ref[0] + gt_ref[0] * acc_ref[...]
        o_ref[0] = _rms(x2, gf_ref[...])


def _ffn(x, sh, sc, gt, g2, gf, w_up_a, w_up_u, w_dw, b_dw, w_down, tm, tf):
    B, L, D = x.shape
    d_ff = w_down.shape[0]
    tm = min(tm, L)
    nt = L // tm
    rpt = tm // GRID_W
    n_rows = L // GRID_W
    tok = lambda b, i, f: (b, i, 0)
    vec = lambda b, i, f: (b, 0, 0)
    return pl.pallas_call(
        functools.partial(_ffn_kernel, tm, nt),
        grid=(B, nt, d_ff // tf),
        in_specs=[pl.BlockSpec((1, tm, D), tok),
                  pl.BlockSpec((1, GRID_W, D), lambda b, i, f: (b, jnp.maximum(i * rpt - 1, 0), 0)),
                  pl.BlockSpec((1, GRID_W, D), lambda b, i, f: (b, jnp.minimum((i + 1) * rpt, n_rows - 1), 0)),
                  pl.BlockSpec((1, 1, D), vec), pl.BlockSpec((1, 1, D), vec), pl.BlockSpec((1, 1, D), vec),
                  pl.BlockSpec((1, D), lambda b, i, f: (0, 0)),
                  pl.BlockSpec((1, D), lambda b, i, f: (0, 0)),
                  pl.BlockSpec((D, tf), lambda b, i, f: (0, f)),
                  pl.BlockSpec((D, tf), lambda b, i, f: (0, f)),
                  pl.BlockSpec((9, tf), lambda b, i, f: (0, f)),
                  pl.BlockSpec((1, tf), lambda b, i, f: (0, f)),
                  pl.BlockSpec((tf, D), lambda b, i, f: (f, 0))],
        out_specs=pl.BlockSpec((1, tm, D), tok),
        out_shape=jax.ShapeDtypeStruct((B, L, D), F32),
        scratch_shapes=[pltpu.VMEM((tm + 2 * GRID_W, D), BF16), pltpu.VMEM((tm, D), F32)],
        compiler_params=_params("arbitrary", "arbitrary", "arbitrary"),
        name="ffn",
    )(x, x, x, sh, sc, gt, g2, gf, w_up_a, w_up_u, w_dw, b_dw, w_down)


def kernel(x, c, ctx, c_ctx, w_mod, b_mod, norm1, norm2, w_in, lb_fwd, lb_bwd, hgrn_norm,
           w_out, w_up, w_dw, b_dw, w_down, norm_f):
    B, L, D = x.shape
    depth = w_mod.shape[0]
    assert depth == 1, "single-layer block"
    l = 0
    d_ff = w_down.shape[1]

    n_mod = -(-(B + 1) // 8) * 8
    cc = jnp.zeros((n_mod, D), F32).at[:B].set(c).at[B].set(c_ctx)
    mod = _modulation(cc, w_mod[l], b_mod[l][None, :])
    sh1, sc1, gt1, sh2, sc2, gt2 = [mod[:, j * D:(j + 1) * D][:, None, :] for j in range(6)]
    lat = lambda m: m[:B]
    ctxrow = lambda m: jnp.broadcast_to(m[B:B + 1], (B, 1, D))

    lb = _lower_bounds(lb_fwd, lb_bwd, l)
    w_in_b = w_in[l].astype(BF16)
    g1 = norm1[l][None, :]

    kf_c, lff_c, kb_c, lfb_c, v_c = _inproj(ctx, ctxrow(sh1), ctxrow(sc1), g1, lb, w_in_b,
                                             (_G_FF, _G_FB, _G_V), tm=256)
    zero = jnp.zeros((B, N_HEADS, HEAD, HEAD), F32)
    s0f, s0b = _gla(None, kf_c, lff_c, kb_c, lfb_c, v_c, zero, zero, need_o=False, tl=256)

    q, kf, lff, kb, lfb, v, sg, u = _inproj(x, lat(sh1), lat(sc1), g1, lb, w_in_b,
                                            (_G_Q, _G_FF, _G_FB, _G_V, _G_G, _G_U), tm=512)
    of, ob, _, _ = _gla(q, kf, lff, kb, lfb, v, s0f, s0b, need_o=True, tl=256)
    yf = _fourier(u)
    x1 = _outproj(x, of, ob, sg, yf, lat(gt1), hgrn_norm[l][None, :], w_out[l].astype(BF16), tm=512)

    w_up_b = w_up[l].astype(BF16)
    return _ffn(x1, lat(sh2), lat(sc2), lat(gt2), norm2[l][None, :], norm_f[None, :],
                w_up_b[:, :d_ff], w_up_b[:, d_ff:], w_dw[l].reshape(9, d_ff), b_dw[l][None, :],
                w_down[l].astype(BF16), tm=1024, tf=256)
```

```python
import functools
import math

import jax
import jax.numpy as jnp
import numpy as np
from jax import lax
from jax.experimental import pallas as pl
from jax.experimental.pallas import tpu as pltpu

EPS = 1e-6
GRID_W = 64
CHUNK = 64
HEAD = 128
N_HEADS = 4
D_HGRN = N_HEADS * HEAD
FOUR_DIM = 128
N_FOUR = 4
DFT_N2 = 64
VMEM_LIMIT = 56 * 1024 * 1024

BF16 = jnp.bfloat16
F32 = jnp.float32


def _params(*sem, flags=None):
    return pltpu.CompilerParams(dimension_semantics=sem, vmem_limit_bytes=VMEM_LIMIT, flags=flags)


def _silu(z):
    return z * (1.0 / (1.0 + jnp.exp(-z)))


def _rms(xf, g):
    ms = jnp.mean(xf * xf, axis=-1, keepdims=True)
    return xf * lax.rsqrt(ms + EPS) * g


def _mod_kernel(c_ref, w_ref, b_ref, o_ref):
    s = _silu(c_ref[...])
    o_ref[...] = jnp.dot(s, w_ref[...], preferred_element_type=F32,
                         precision=lax.Precision.HIGHEST) + b_ref[...]


def _modulation(cc, w_mod, b_mod):
    n, d = cc.shape
    n_out = w_mod.shape[1]
    tn = 1024
    return pl.pallas_call(
        _mod_kernel,
        grid=(n_out // tn,),
        in_specs=[pl.BlockSpec((n, d), lambda j: (0, 0)),
                  pl.BlockSpec((d, tn), lambda j: (0, j)),
                  pl.BlockSpec((1, tn), lambda j: (0, j))],
        out_specs=pl.BlockSpec((n, tn), lambda j: (0, j)),
        out_shape=jax.ShapeDtypeStruct((n, n_out), F32),
        compiler_params=_params("arbitrary"),
        name="mod",
    )(cc, w_mod, b_mod)


def _lb_kernel(layer, f_ref, b_ref, o_ref):
    def table(ref):
        t = ref[...]
        e = jnp.exp(t - jnp.max(t, axis=0, keepdims=True))
        p = e / jnp.sum(e, axis=0, keepdims=True)
        return jnp.sum(p[:layer + 1], axis=0, keepdims=True)
    o_ref[0:1, :] = table(f_ref)
    o_ref[1:2, :] = table(b_ref)


def _lower_bounds(lb_fwd, lb_bwd, layer):
    return pl.pallas_call(
        functools.partial(_lb_kernel, layer),
        out_shape=jax.ShapeDtypeStruct((2, lb_fwd.shape[1]), F32),
        name="lb",
    )(lb_fwd, lb_bwd)


_G_Q, _G_FF, _G_FB, _G_V, _G_G, _G_U = range(6)


def _inproj_kernel(groups, x_ref, sh_ref, sc_ref, g_ref, lb_ref, w_ref, *out_refs):
    xf = x_ref[0]
    h = (_rms(xf, g_ref[...]) * (1.0 + sc_ref[0]) + sh_ref[0]).astype(BF16)
    outs = iter(out_refs)
    for grp in groups:
        z = jnp.dot(h, w_ref[:, grp * D_HGRN:(grp + 1) * D_HGRN], preferred_element_type=F32)
        if grp in (_G_Q, _G_G):
            o = next(outs)
            o[0] = _silu(z).astype(o.dtype)
        elif grp in (_G_FF, _G_FB):
            lb = lb_ref[0:1, :] if grp == _G_FF else lb_ref[1:2, :]
            f = lb + (1.0 - lb) * (1.0 / (1.0 + jnp.exp(-z)))
            ko, lo = next(outs), next(outs)
            ko[0] = (1.0 - f).astype(ko.dtype)
            lo[0] = jnp.log(f)
        else:
            o = next(outs)
            o[0] = z.astype(o.dtype)


def _inproj(x, sh, sc, g, lb, w_in, groups, tm):
    B, L, D = x.shape
    tm = min(tm, L)
    out_shape, out_specs = [], []
    tok = lambda b, i: (b, i, 0)
    for grp in groups:
        dts = (BF16, F32) if grp in (_G_FF, _G_FB) else (BF16,)
        for dt in dts:
            out_shape.append(jax.ShapeDtypeStruct((B, L, D_HGRN), dt))
            out_specs.append(pl.BlockSpec((1, tm, D_HGRN), tok))
    return pl.pallas_call(
        functools.partial(_inproj_kernel, groups),
        grid=(B, L // tm),
        in_specs=[pl.BlockSpec((1, tm, D), tok),
                  pl.BlockSpec((1, 1, D), lambda b, i: (b, 0, 0)),
                  pl.BlockSpec((1, 1, D), lambda b, i: (b, 0, 0)),
                  pl.BlockSpec((1, D), lambda b, i: (0, 0)),
                  pl.BlockSpec(lb.shape, lambda b, i: (0, 0)),
                  pl.BlockSpec(w_in.shape, lambda b, i: (0, 0))],
        out_specs=out_specs,
        out_shape=out_shape,
        compiler_params=_params("arbitrary", "arbitrary"),
        name="inproj",
    )(x, sh, sc, g, lb, w_in)


def _gla_direction(rev, need_o, nc, q_ref, k_ref, lf_ref, v_ref, st_ref, o_ref, scr):
    cum_s, qt_s, kt_s, qi_s, kd_s, p_s, u_s, sp_s = scr
    tl = nc * CHUNK
    nt_dims = (((1,), (1,)), ((), ()))
    order = range(nc - 1, -1, -1) if rev else range(nc)
    i_ref = CHUNK - 1 - CHUNK // 2 if rev else CHUNK // 2
    i_last = 0 if rev else CHUNK - 1
    pieces = [(c, h, slice(c * CHUNK, (c + 1) * CHUNK), slice(h * HEAD, (h + 1) * HEAD))
              for c in order for h in range(N_HEADS)]

    r = lax.broadcasted_iota(jnp.int32, (tl, tl), 0)
    c_ = lax.broadcasted_iota(jnp.int32, (tl, tl), 1)
    causal = (r <= c_) if rev else (r >= c_)
    tri = (causal & ((r // CHUNK) == (c_ // CHUNK))).astype(BF16)
    lf = lf_ref[0]
    hi = lf.astype(BF16)
    lo = (lf - hi.astype(F32)).astype(BF16)
    cum_s[...] = (jnp.dot(tri, hi, preferred_element_type=F32)
                  + jnp.dot(tri, lo, preferred_element_type=F32))

    yield
    decay = {}
    for c, h, rows, cols in pieces:
        bh = cum_s[rows, cols]
        b_ref = bh[i_ref:i_ref + 1]
        b_last = bh[i_last:i_last + 1]
        kh = k_ref[0, rows, cols].astype(F32) * jnp.exp(b_ref - bh)
        kd_s[rows, cols] = (kh * jnp.exp(b_last - b_ref)).astype(BF16)
        decay[c, h] = jnp.exp(b_last)
        if need_o:
            kt_s[rows, cols] = kh.astype(BF16)
            qh = q_ref[0, rows, cols].astype(F32) * jnp.exp(bh - b_ref)
            qt_s[rows, cols] = qh.astype(BF16)
            qi_s[rows, cols] = (qh * jnp.exp(b_ref)).astype(BF16)

    yield
    keep = causal[:CHUNK, :CHUNK]
    for c, h, rows, cols in pieces:
        u_s[c, h] = lax.dot_general(v_ref[0, rows, cols], kd_s[rows, cols], (((0,), (0,)), ((), ())),
                                    preferred_element_type=F32)
        if need_o:
            s = lax.dot_general(qt_s[rows, cols], kt_s[rows, cols], nt_dims, preferred_element_type=F32)
            p_s[rows, h * CHUNK:(h + 1) * CHUNK] = jnp.where(keep, s, 0.0).astype(BF16)

    yield
    for h in range(N_HEADS):
        st = st_ref[0, h]
        for c in order:
            if need_o:
                sp_s[c, h] = st.astype(BF16)
            st = decay[c, h] * st + u_s[c, h]
        st_ref[0, h] = st

    yield
    if need_o:
        for c, h, rows, cols in pieces:
            o = jnp.dot(p_s[rows, h * CHUNK:(h + 1) * CHUNK], v_ref[0, rows, cols], preferred_element_type=F32)
            o = o + lax.dot_general(qi_s[rows, cols], sp_s[c, h], nt_dims, preferred_element_type=F32)
            o_ref[0, rows, cols] = o.astype(o_ref.dtype)
    yield


def _gla_scratch(tl):
    nc = tl // CHUNK
    wide = lambda dt: pltpu.VMEM((tl, D_HGRN), dt)
    return [wide(F32), wide(BF16), wide(BF16), wide(BF16), wide(BF16),
            pltpu.VMEM((tl, N_HEADS * CHUNK), BF16),
            pltpu.VMEM((nc, N_HEADS, HEAD, HEAD), F32),
            pltpu.VMEM((nc, N_HEADS, HEAD, HEAD), BF16)]


def _gla_kernel(need_o, nc, *refs):
    n_scr = len(_gla_scratch(CHUNK))
    scr_f, scr_b = refs[-2 * n_scr:-n_scr], refs[-n_scr:]
    refs = refs[:-2 * n_scr]
    if need_o:
        (qf, kf, lff, vf, qb, kb, lfb, vb, s0f, s0b, of, ob, sf, sb) = refs
    else:
        (kf, lff, vf, kb, lfb, vb, s0f, s0b, sf, sb) = refs
        qf = qb = of = ob = None

    @pl.when(pl.program_id(1) == 0)
    def _():
        sf[...] = s0f[...]
        sb[...] = s0b[...]

    for _ in zip(_gla_direction(False, need_o, nc, qf, kf, lff, vf, sf, of, scr_f),
                 _gla_direction(True, need_o, nc, qb, kb, lfb, vb, sb, ob, scr_b)):
        pass


def _gla(q, kf, lff, kb, lfb, v, s0f, s0b, need_o, tl):
    B, L, _ = v.shape
    tl = min(tl, L)
    nt = L // tl
    fwd = pl.BlockSpec((1, tl, D_HGRN), lambda b, i: (b, i, 0))
    bwd = pl.BlockSpec((1, tl, D_HGRN), lambda b, i: (b, nt - 1 - i, 0))
    st = pl.BlockSpec((1, N_HEADS, HEAD, HEAD), lambda b, i: (b, 0, 0, 0))
    st_shape = jax.ShapeDtypeStruct((B, N_HEADS, HEAD, HEAD), F32)
    if need_o:
        ins = (q, kf, lff, v, q, kb, lfb, v, s0f, s0b)
        in_specs = [fwd] * 4 + [bwd] * 4 + [st, st]
        out_specs = [fwd, bwd, st, st]
        o_shape = jax.ShapeDtypeStruct((B, L, D_HGRN), BF16)
        out_shape = [o_shape, o_shape, st_shape, st_shape]
    else:
        ins = (kf, lff, v, kb, lfb, v, s0f, s0b)
        in_specs = [fwd] * 3 + [bwd] * 3 + [st, st]
        out_specs = [st, st]
        out_shape = [st_shape, st_shape]
    return pl.pallas_call(
        functools.partial(_gla_kernel, need_o, tl // CHUNK),
        grid=(B, nt),
        in_specs=in_specs,
        out_specs=out_specs,
        out_shape=out_shape,
        scratch_shapes=_gla_scratch(tl) + _gla_scratch(tl),
        compiler_params=_params("arbitrary", "arbitrary"),
        name="gla",
    )(*ins)


def _dft_tables(L):
    n1, n2 = L // DFT_N2, DFT_N2
    scale = 1.0 / math.sqrt(L * FOUR_DIM)
    k1 = np.arange(n1)[:, None]
    t1 = np.arange(n1)[None, :]
    ang = 2.0 * np.pi * ((k1 * t1) % n1) / n1
    fa = np.concatenate([np.cos(ang), -np.sin(ang)], axis=0)
    t2 = np.arange(n2)[:, None]
    angw = 2.0 * np.pi * ((t2 * np.arange(n1)[None, :]) % L) / L
    cw = np.broadcast_to(np.cos(angw)[:, :, None], (n2, n1, 128))
    sw = np.broadcast_to(np.sin(angw)[:, :, None], (n2, n1, 128))
    k2 = np.arange(n2)[:, None]
    ang2 = 2.0 * np.pi * ((k2 * np.arange(n2)[None, :]) % n2) / n2
    c2, s2 = np.cos(ang2), np.sin(ang2)
    fb = np.block([[c2, s2], [-s2, c2]])
    n = np.arange(FOUR_DIM)[:, None]
    angc = 2.0 * np.pi * ((n * np.arange(FOUR_DIM)[None, :]) % FOUR_DIM) / FOUR_DIM
    ch = np.concatenate([np.cos(angc), np.sin(angc)], axis=0) * scale
    f32 = lambda a: jnp.asarray(a, F32)
    return (f32(fa).astype(BF16), f32(cw), f32(sw), f32(fb).astype(BF16), f32(ch).astype(BF16))


def _dft_a_kernel(n1, n_t2, fa_ref, cw_ref, sw_ref, u_ref, o_ref):
    v = jnp.dot(fa_ref[...], u_ref[0], preferred_element_type=F32)
    for j in range(n_t2):
        cw, sw = cw_ref[j], sw_ref[j]
        for g in range(N_FOUR):
            cols = slice(j * N_FOUR * FOUR_DIM + g * FOUR_DIM, j * N_FOUR * FOUR_DIM + (g + 1) * FOUR_DIM)
            vr, vi = v[:n1, cols], v[n1:, cols]
            o_ref[0, :n1, cols] = (vr * cw + vi * sw).astype(o_ref.dtype)
            o_ref[0, n1:, cols] = (vi * cw - vr * sw).astype(o_ref.dtype)


def _dft_b_kernel(n_k1, fb_ref, ch_ref, re_ref, im_ref, o_ref):
    n2 = DFT_N2
    d = N_FOUR * FOUR_DIM
    for j in range(n_k1):
        stack = jnp.concatenate([re_ref[0, j], im_ref[0, j]], axis=0)
        x = jnp.dot(fb_ref[...], stack, preferred_element_type=F32)
        for g in range(N_FOUR):
            cols = slice(g * FOUR_DIM, (g + 1) * FOUR_DIM)
            xg = jnp.concatenate([x[:n2, cols], x[n2:, cols]], axis=1).astype(BF16)
            y = jnp.dot(xg, ch_ref[...], preferred_element_type=F32)
            o_ref[0, :, j * d + g * FOUR_DIM:j * d + (g + 1) * FOUR_DIM] = y.astype(o_ref.dtype)


def _fourier(u):
    B, L, D = u.shape
    n1, n2 = L // DFT_N2, DFT_N2
    fa, cw, sw, fb, ch = _dft_tables(L)
    n_t2 = 8
    const2 = lambda b, i: (0, 0)
    v = pl.pallas_call(
        functools.partial(_dft_a_kernel, n1, n_t2),
        grid=(B, n2 // n_t2),
        in_specs=[pl.BlockSpec(fa.shape, const2),
                  pl.BlockSpec((n_t2, n1, 128), lambda b, i: (i, 0, 0)),
                  pl.BlockSpec((n_t2, n1, 128), lambda b, i: (i, 0, 0)),
                  pl.BlockSpec((1, n1, n_t2 * D), lambda b, i: (b, 0, i))],
        out_specs=pl.BlockSpec((1, 2 * n1, n_t2 * D), lambda b, i: (b, 0, i)),
        out_shape=jax.ShapeDtypeStruct((B, 2 * n1, n2 * D), BF16),
        compiler_params=_params("arbitrary", "arbitrary"),
        name="dft_a",
    )(fa, cw, sw, u.reshape(B, n1, n2 * D))
    v = v.reshape(B, 2 * n1, n2, D)
    n_k1 = min(8, n1)
    nb = n1 // n_k1
    y = pl.pallas_call(
        functools.partial(_dft_b_kernel, n_k1),
        grid=(B, nb),
        in_specs=[pl.BlockSpec(fb.shape, const2),
                  pl.BlockSpec(ch.shape, const2),
                  pl.BlockSpec((1, n_k1, n2, D), lambda b, i: (b, i, 0, 0)),
                  pl.BlockSpec((1, n_k1, n2, D), lambda b, i: (b, nb + i, 0, 0))],
        out_specs=pl.BlockSpec((1, n2, n_k1 * D), lambda b, i: (b, 0, i)),
        out_shape=jax.ShapeDtypeStruct((B, n2, n1 * D), BF16),
        compiler_params=_params("arbitrary", "arbitrary"),
        name="dft_b",
    )(fb, ch, v, v)
    return y.reshape(B, L, D)


def _outproj_kernel(x_ref, of_ref, ob_ref, sg_ref, yf_ref, gt_ref, gn_ref, w_ref, o_ref):
    o = of_ref[0].astype(F32) + ob_ref[0].astype(F32)
    parts = []
    for h in range(N_HEADS):
        cols = slice(h * HEAD, (h + 1) * HEAD)
        parts.append(_rms(o[:, cols], gn_ref[...]))
    o = jnp.concatenate(parts, axis=1) * sg_ref[0].astype(F32)
    mix = jnp.concatenate([o.astype(BF16), yf_ref[0]], axis=1)
    y = jnp.dot(mix, w_ref[...], preferred_element_type=F32)
    o_ref[0] = x_ref[0] + gt_ref[0] * y


def _outproj(x, of, ob, sg, yf, gt1, gn, w_out, tm):
    B, L, D = x.shape
    tm = min(tm, L)
    tok = lambda b, i: (b, i, 0)
    half = pl.BlockSpec((1, tm, D_HGRN), tok)
    return pl.pallas_call(
        _outproj_kernel,
        grid=(B, L // tm),
        in_specs=[pl.BlockSpec((1, tm, D), tok), half, half, half, half,
                  pl.BlockSpec((1, 1, D), lambda b, i: (b, 0, 0)),
                  pl.BlockSpec((1, HEAD), lambda b, i: (0, 0)),
                  pl.BlockSpec(w_out.shape, lambda b, i: (0, 0))],
        out_specs=pl.BlockSpec((1, tm, D), tok),
        out_shape=jax.ShapeDtypeStruct((B, L, D), F32),
        compiler_params=_params("arbitrary", "arbitrary"),
        name="outproj",
    )(x, of, ob, sg, yf, gt1, gn, w_out)


def _gelu_tanh(a):
    return 0.5 * a * (1.0 + jnp.tanh(0.7978845608028654 * (a + 0.044715 * (a * a * a))))


def _ffn_kernel(tm, nt, tf, x_ref, top_ref, bot_ref, sh_ref, sc_ref, gt_ref, g2_ref, gf_ref,
                wa_ref, wu_ref, wdw_ref, bdw_ref, wd_ref, o_ref, h_ref, a0, a1, u0, u1, g0, g1, acc_ref):
    a_scr, u_scr, g_scr = (a0, a1), (u0, u1), (g0, g1)
    i = pl.program_id(1)
    W = GRID_W
    n_rows = tm + 2 * W

    def prep(xf):
        return _rms(xf, g2_ref[...]) * (1.0 + sc_ref[0]) + sh_ref[0]

    h_ref[0:W, :] = jnp.where(i == 0, 0.0, prep(top_ref[0])).astype(BF16)
    h_ref[W:W + tm, :] = prep(x_ref[0]).astype(BF16)
    h_ref[W + tm:, :] = jnp.where(i == nt - 1, 0.0, prep(bot_ref[0])).astype(BF16)

    acc_ref[...] = jnp.zeros_like(acc_ref)
    nf = wd_ref.shape[0]
    Q = 4
    tq = tm // Q
    ta = n_rows // Q
    colx = lax.broadcasted_iota(jnp.int32, (W, 1), 0)
    first_col = colx == 0
    last_col = colx == W - 1

    def up(f, slot, q):
        a_scr[slot][q * ta:(q + 1) * ta, :] = jnp.dot(
            h_ref[q * ta:(q + 1) * ta, :], wa_ref[f], preferred_element_type=F32)
        u_scr[slot][q * tq:(q + 1) * tq, :] = jnp.dot(
            h_ref[W + q * tq:W + (q + 1) * tq, :], wu_ref[f], preferred_element_type=F32)

    def gate(f, slot, q):
        for r in range(q * tq, (q + 1) * tq, W):
            for lanes in (slice(l, l + 128) for l in range(0, tf, 128)):
                taps = [a_scr[slot][r + dr * W:r + (dr + 1) * W, lanes] for dr in range(3)]
                w = lambda k: wdw_ref[f, k:k + 1, lanes]
                c = [w(j) * taps[0] + w(3 + j) * taps[1] + w(6 + j) * taps[2] for j in range(3)]
                conv = (jnp.where(first_col, 0.0, pltpu.roll(c[0], 1, axis=0)) + c[1]
                        + jnp.where(last_col, 0.0, pltpu.roll(c[2], W - 1, axis=0))
                        + bdw_ref[f, :, lanes])
                g_scr[slot][r:r + W, lanes] = (_gelu_tanh(conv) * u_scr[slot][r:r + W, lanes]).astype(BF16)

    def down(f, slot, q):
        acc_ref[q * tq:(q + 1) * tq, :] += jnp.dot(
            g_scr[slot][q * tq:(q + 1) * tq, :], wd_ref[f], preferred_element_type=F32)

    def step(f, slot, has_down=True, has_up=True):
        for q in range(Q):
            if has_down:
                down(f - 1, 1 - slot, q)
            if has_up:
                up(f + 1, 1 - slot, q)
            gate(f, slot, q)

    for q in range(Q):
        up(0, 0, q)
    step(0, 0, has_down=False)

    def body(p, carry):
        step(2 * p + 1, 1)
        step(2 * p + 2, 0)
        return carry

    n_pairs = (nf - 2) // 2
    lax.fori_loop(0, n_pairs, body, 0)
    for f in range(2 * n_pairs + 1, nf):
        step(f, f % 2, has_up=f + 1 < nf)
    for q in range(Q):
        down(nf - 1, (nf - 1) % 2, q)
    x2 = x_ref[0] + gt_ref[0] * acc_ref[...]
    o_ref[0] = _rms(x2, gf_ref[...])


def _ffn(x, sh, sc, gt, g2, gf, w_up_a, w_up_u, w_dw, b_dw, w_down, tm, tf):
    B, L, D = x.shape
    d_ff = w_down.shape[0]
    tm = min(tm, L)
    nt = L // tm
    rpt = tm // GRID_W
    n_rows = L // GRID_W
    tok = lambda b, i: (b, i, 0)
    vec = lambda b, i: (b, 0, 0)
    nf = d_ff // tf
    resident = lambda a: pl.BlockSpec(a.shape, lambda b, i: (0, 0, 0), pipeline_mode=pl.Buffered(1))
    w_up_a = w_up_a.reshape(D, nf, tf).transpose(1, 0, 2)
    w_up_u = w_up_u.reshape(D, nf, tf).transpose(1, 0, 2)
    w_dw = w_dw.reshape(9, nf, tf).transpose(1, 0, 2)
    b_dw = b_dw.reshape(nf, 1, tf)
    w_down = w_down.reshape(nf, tf, D)
    return pl.pallas_call(
        functools.partial(_ffn_kernel, tm, nt, tf),
        grid=(B, nt),
        in_specs=[pl.BlockSpec((1, tm, D), tok),
                  pl.BlockSpec((1, GRID_W, D), lambda b, i: (b, jnp.maximum(i * rpt - 1, 0), 0)),
                  pl.BlockSpec((1, GRID_W, D), lambda b, i: (b, jnp.minimum((i + 1) * rpt, n_rows - 1), 0)),
                  pl.BlockSpec((1, 1, D), vec), pl.BlockSpec((1, 1, D), vec), pl.BlockSpec((1, 1, D), vec),
                  pl.BlockSpec((1, D), lambda b, i: (0, 0)),
                  pl.BlockSpec((1, D), lambda b, i: (0, 0)),
                  resident(w_up_a), resident(w_up_u), resident(w_dw), resident(b_dw), resident(w_down)],
        out_specs=pl.BlockSpec((1, tm, D), tok),
        out_shape=jax.ShapeDtypeStruct((B, L, D), F32),
        scratch_shapes=[pltpu.VMEM((tm + 2 * GRID_W, D), BF16),
                        pltpu.VMEM((tm + 2 * GRID_W, tf), F32), pltpu.VMEM((tm + 2 * GRID_W, tf), F32),
                        pltpu.VMEM((tm, tf), F32), pltpu.VMEM((tm, tf), F32),
                        pltpu.VMEM((tm, tf), BF16), pltpu.VMEM((tm, tf), BF16),
                        pltpu.VMEM((tm, D), F32)],
        compiler_params=_params("arbitrary", "arbitrary"),
        name="ffn",
    )(x, x, x, sh, sc, gt, g2, gf, w_up_a, w_up_u, w_dw, b_dw, w_down)


def kernel(x, c, ctx, c_ctx, w_mod, b_mod, norm1, norm2, w_in, lb_fwd, lb_bwd, hgrn_norm,
           w_out, w_up, w_dw, b_dw, w_down, norm_f):
    B, L, D = x.shape
    depth = w_mod.shape[0]
    assert depth == 1, "single-layer block"
    l = 0
    d_ff = w_down.shape[1]

    n_mod = -(-(B + 1) // 8) * 8
    cc = jnp.zeros((n_mod, D), F32).at[:B].set(c).at[B].set(c_ctx)
    mod = _modulation(cc, w_mod[l], b_mod[l][None, :])
    sh1, sc1, gt1, sh2, sc2, gt2 = [mod[:, j * D:(j + 1) * D][:, None, :] for j in range(6)]
    lat = lambda m: m[:B]
    ctxrow = lambda m: jnp.broadcast_to(m[B:B + 1], (B, 1, D))

    lb = _lower_bounds(lb_fwd, lb_bwd, l)
    w_in_b = w_in[l].astype(BF16)
    g1 = norm1[l][None, :]

    kf_c, lff_c, kb_c, lfb_c, v_c = _inproj(ctx, ctxrow(sh1), ctxrow(sc1), g1, lb, w_in_b,
                                             (_G_FF, _G_FB, _G_V), tm=256)
    zero = jnp.zeros((B, N_HEADS, HEAD, HEAD), F32)
    s0f, s0b = _gla(None, kf_c, lff_c, kb_c, lfb_c, v_c, zero, zero, need_o=False, tl=256)

    q, kf, lff, kb, lfb, v, sg, u = _inproj(x, lat(sh1), lat(sc1), g1, lb, w_in_b,
                                            (_G_Q, _G_FF, _G_FB, _G_V, _G_G, _G_U), tm=512)
    of, ob, _, _ = _gla(q, kf, lff, kb, lfb, v, s0f, s0b, need_o=True, tl=256)
    yf = _fourier(u)
    x1 = _outproj(x, of, ob, sg, yf, lat(gt1), hgrn_norm[l][None, :], w_out[l].astype(BF16), tm=512)

    w_up_b = w_up[l].astype(BF16)
    return _ffn(x1, lat(sh2), lat(sc2), lat(gt2), norm2[l][None, :], norm_f[None, :],
                w_up_b[:, :d_ff], w_up_b[:, d_ff:], w_dw[l].reshape(9, d_ff), b_dw[l][None, :],
                w_down[l].astype(BF16), tm=1024, tf=256)
```

```python
import functools
import math

import jax
import jax.numpy as jnp
import numpy as np
from jax import lax
from jax.experimental import pallas as pl
from jax.experimental.pallas import tpu as pltpu

EPS = 1e-6
GRID_W = 64
CHUNK = 64
HEAD = 128
N_HEADS = 4
D_HGRN = N_HEADS * HEAD
FOUR_DIM = 128
N_FOUR = 4
DFT_N2 = 64
VMEM_LIMIT = 56 * 1024 * 1024

BF16 = jnp.bfloat16
F32 = jnp.float32


def _params(*sem, flags=None):
    return pltpu.CompilerParams(dimension_semantics=sem, vmem_limit_bytes=VMEM_LIMIT, flags=flags)


def _silu(z):
    return z * (1.0 / (1.0 + jnp.exp(-z)))


def _rms(xf, g):
    ms = jnp.mean(xf * xf, axis=-1, keepdims=True)
    return xf * lax.rsqrt(ms + EPS) * g


def _mod_kernel(c_ref, w_ref, b_ref, o_ref):
    s = _silu(c_ref[...])
    o_ref[...] = jnp.dot(s, w_ref[...], preferred_element_type=F32,
                         precision=lax.Precision.HIGHEST) + b_ref[...]


def _modulation(cc, w_mod, b_mod):
    n, d = cc.shape
    n_out = w_mod.shape[1]
    tn = 1024
    return pl.pallas_call(
        _mod_kernel,
        grid=(n_out // tn,),
        in_specs=[pl.BlockSpec((n, d), lambda j: (0, 0)),
                  pl.BlockSpec((d, tn), lambda j: (0, j)),
                  pl.BlockSpec((1, tn), lambda j: (0, j))],
        out_specs=pl.BlockSpec((n, tn), lambda j: (0, j)),
        out_shape=jax.ShapeDtypeStruct((n, n_out), F32),
        compiler_params=_params("arbitrary"),
        name="mod",
    )(cc, w_mod, b_mod)


def _lb_kernel(layer, f_ref, b_ref, o_ref):
    def table(ref):
        t = ref[...]
        e = jnp.exp(t - jnp.max(t, axis=0, keepdims=True))
        p = e / jnp.sum(e, axis=0, keepdims=True)
        return jnp.sum(p[:layer + 1], axis=0, keepdims=True)
    o_ref[0:1, :] = table(f_ref)
    o_ref[1:2, :] = table(b_ref)


def _lower_bounds(lb_fwd, lb_bwd, layer):
    return pl.pallas_call(
        functools.partial(_lb_kernel, layer),
        out_shape=jax.ShapeDtypeStruct((2, lb_fwd.shape[1]), F32),
        name="lb",
    )(lb_fwd, lb_bwd)


_G_Q, _G_FF, _G_FB, _G_V, _G_G, _G_U = range(6)


def _inproj_kernel(groups, x_ref, sh_ref, sc_ref, g_ref, lb_ref, w_ref, *out_refs):
    xf = x_ref[0]
    h = (_rms(xf, g_ref[...]) * (1.0 + sc_ref[0]) + sh_ref[0]).astype(BF16)
    outs = iter(out_refs)
    for grp in groups:
        z = jnp.dot(h, w_ref[:, grp * D_HGRN:(grp + 1) * D_HGRN], preferred_element_type=F32)
        if grp in (_G_Q, _G_G):
            o = next(outs)
            o[0] = _silu(z).astype(o.dtype)
        elif grp in (_G_FF, _G_FB):
            lb = lb_ref[0:1, :] if grp == _G_FF else lb_ref[1:2, :]
            f = lb + (1.0 - lb) * (1.0 / (1.0 + jnp.exp(-z)))
            ko, lo = next(outs), next(outs)
            ko[0] = (1.0 - f).astype(ko.dtype)
            lo[0] = jnp.log(f)
        else:
            o = next(outs)
            o[0] = z.astype(o.dtype)


def _inproj(x, sh, sc, g, lb, w_in, groups, tm):
    B, L, D = x.shape
    tm = min(tm, L)
    out_shape, out_specs = [], []
    tok = lambda b, i: (b, i, 0)
    for grp in groups:
        dts = (BF16, F32) if grp in (_G_FF, _G_FB) else (BF16,)
        for dt in dts:
            out_shape.append(jax.ShapeDtypeStruct((B, L, D_HGRN), dt))
            out_specs.append(pl.BlockSpec((1, tm, D_HGRN), tok))
    return pl.pallas_call(
        functools.partial(_inproj_kernel, groups),
        grid=(B, L // tm),
        in_specs=[pl.BlockSpec((1, tm, D), tok),
                  pl.BlockSpec((1, 1, D), lambda b, i: (b, 0, 0)),
                  pl.BlockSpec((1, 1, D), lambda b, i: (b, 0, 0)),
                  pl.BlockSpec((1, D), lambda b, i: (0, 0)),
                  pl.BlockSpec(lb.shape, lambda b, i: (0, 0)),
                  pl.BlockSpec(w_in.shape, lambda b, i: (0, 0))],
        out_specs=out_specs,
        out_shape=out_shape,
        compiler_params=_params("arbitrary", "arbitrary"),
        name="inproj",
    )(x, sh, sc, g, lb, w_in)


def _gla_direction(rev, need_o, nc, q_ref, k_ref, lf_ref, v_ref, st_ref, o_ref, scr):
    cum_s, qt_s, kt_s, qi_s, kd_s, p_s, u_s, sp_s = scr
    tl = nc * CHUNK
    nt_dims = (((1,), (1,)), ((), ()))
    order = range(nc - 1, -1, -1) if rev else range(nc)
    i_ref = CHUNK - 1 - CHUNK // 2 if rev else CHUNK // 2
    i_last = 0 if rev else CHUNK - 1
    pieces = [(c, h, slice(c * CHUNK, (c + 1) * CHUNK), slice(h * HEAD, (h + 1) * HEAD))
              for c in order for h in range(N_HEADS)]

    r = lax.broadcasted_iota(jnp.int32, (tl, tl), 0)
    c_ = lax.broadcasted_iota(jnp.int32, (tl, tl), 1)
    causal = (r <= c_) if rev else (r >= c_)
    tri = (causal & ((r // CHUNK) == (c_ // CHUNK))).astype(BF16)
    lf = lf_ref[0]
    hi = lf.astype(BF16)
    lo = (lf - hi.astype(F32)).astype(BF16)
    cum_s[...] = (jnp.dot(tri, hi, preferred_element_type=F32)
                  + jnp.dot(tri, lo, preferred_element_type=F32))

    yield
    decay = {}
    for c, h, rows, cols in pieces:
        bh = cum_s[rows, cols]
        b_ref = bh[i_ref:i_ref + 1]
        b_last = bh[i_last:i_last + 1]
        kh = k_ref[0, rows, cols].astype(F32) * jnp.exp(b_ref - bh)
        kd_s[rows, cols] = (kh * jnp.exp(b_last - b_ref)).astype(BF16)
        decay[c, h] = jnp.exp(b_last)
        if need_o:
            kt_s[rows, cols] = kh.astype(BF16)
            qh = q_ref[0, rows, cols].astype(F32) * jnp.exp(bh - b_ref)
            qt_s[rows, cols] = qh.astype(BF16)
            qi_s[rows, cols] = (qh * jnp.exp(b_ref)).astype(BF16)

    yield
    keep = causal[:CHUNK, :CHUNK]
    for c, h, rows, cols in pieces:
        u_s[c, h] = lax.dot_general(v_ref[0, rows, cols], kd_s[rows, cols], (((0,), (0,)), ((), ())),
                                    preferred_element_type=F32)
        if need_o:
            s = lax.dot_general(qt_s[rows, cols], kt_s[rows, cols], nt_dims, preferred_element_type=F32)
            p_s[rows, h * CHUNK:(h + 1) * CHUNK] = jnp.where(keep, s, 0.0).astype(BF16)

    yield
    for h in range(N_HEADS):
        st = st_ref[0, h]
        for c in order:
            if need_o:
                sp_s[c, h] = st.astype(BF16)
            st = decay[c, h] * st + u_s[c, h]
        st_ref[0, h] = st

    yield
    if need_o:
        for c, h, rows, cols in pieces:
            o = jnp.dot(p_s[rows, h * CHUNK:(h + 1) * CHUNK], v_ref[0, rows, cols], preferred_element_type=F32)
            o = o + lax.dot_general(qi_s[rows, cols], sp_s[c, h], nt_dims, preferred_element_type=F32)
            o_ref[0, rows, cols] = o.astype(o_ref.dtype)
    yield


def _gla_scratch(tl):
    nc = tl // CHUNK
    wide = lambda dt: pltpu.VMEM((tl, D_HGRN), dt)
    return [wide(F32), wide(BF16), wide(BF16), wide(BF16), wide(BF16),
            pltpu.VMEM((tl, N_HEADS * CHUNK), BF16),
            pltpu.VMEM((nc, N_HEADS, HEAD, HEAD), F32),
            pltpu.VMEM((nc, N_HEADS, HEAD, HEAD), BF16)]


def _gla_kernel(need_o, nc, *refs):
    n_scr = len(_gla_scratch(CHUNK))
    scr_f, scr_b = refs[-2 * n_scr:-n_scr], refs[-n_scr:]
    refs = refs[:-2 * n_scr]
    if need_o:
        (qf, kf, lff, vf, qb, kb, lfb, vb, s0f, s0b, of, ob, sf, sb) = refs
    else:
        (kf, lff, vf, kb, lfb, vb, s0f, s0b, sf, sb) = refs
        qf = qb = of = ob = None

    @pl.when(pl.program_id(1) == 0)
    def _():
        sf[...] = s0f[...]
        sb[...] = s0b[...]

    for _ in zip(_gla_direction(False, need_o, nc, qf, kf, lff, vf, sf, of, scr_f),
                 _gla_direction(True, need_o, nc, qb, kb, lfb, vb, sb, ob, scr_b)):
        pass


def _gla(q, kf, lff, kb, lfb, v, s0f, s0b, need_o, tl):
    B, L, _ = v.shape
    tl = min(tl, L)
    nt = L // tl
    fwd = pl.BlockSpec((1, tl, D_HGRN), lambda b, i: (b, i, 0))
    bwd = pl.BlockSpec((1, tl, D_HGRN), lambda b, i: (b, nt - 1 - i, 0))
    st = pl.BlockSpec((1, N_HEADS, HEAD, HEAD), lambda b, i: (b, 0, 0, 0))
    st_shape = jax.ShapeDtypeStruct((B, N_HEADS, HEAD, HEAD), F32)
    if need_o:
        ins = (q, kf, lff, v, q, kb, lfb, v, s0f, s0b)
        in_specs = [fwd] * 4 + [bwd] * 4 + [st, st]
        out_specs = [fwd, bwd, st, st]
        o_shape = jax.ShapeDtypeStruct((B, L, D_HGRN), BF16)
        out_shape = [o_shape, o_shape, st_shape, st_shape]
    else:
        ins = (kf, lff, v, kb, lfb, v, s0f, s0b)
        in_specs = [fwd] * 3 + [bwd] * 3 + [st, st]
        out_specs = [st, st]
        out_shape = [st_shape, st_shape]
    return pl.pallas_call(
        functools.partial(_gla_kernel, need_o, tl // CHUNK),
        grid=(B, nt),
        in_specs=in_specs,
        out_specs=out_specs,
        out_shape=out_shape,
        scratch_shapes=_gla_scratch(tl) + _gla_scratch(tl),
        compiler_params=_params("arbitrary", "arbitrary"),
        name="gla",
    )(*ins)


def _dft_tables(L):
    n1, n2 = L // DFT_N2, DFT_N2
    scale = 1.0 / math.sqrt(L * FOUR_DIM)
    k1 = np.arange(n1)[:, None]
    t1 = np.arange(n1)[None, :]
    ang = 2.0 * np.pi * ((k1 * t1) % n1) / n1
    fa = np.concatenate([np.cos(ang), -np.sin(ang)], axis=0)
    t2 = np.arange(n2)[:, None]
    angw = 2.0 * np.pi * ((t2 * np.arange(n1)[None, :]) % L) / L
    cw = np.broadcast_to(np.cos(angw)[:, :, None], (n2, n1, 128))
    sw = np.broadcast_to(np.sin(angw)[:, :, None], (n2, n1, 128))
    k2 = np.arange(n2)[:, None]
    ang2 = 2.0 * np.pi * ((k2 * np.arange(n2)[None, :]) % n2) / n2
    c2, s2 = np.cos(ang2), np.sin(ang2)
    fb = np.block([[c2, s2], [-s2, c2]])
    n = np.arange(FOUR_DIM)[:, None]
    angc = 2.0 * np.pi * ((n * np.arange(FOUR_DIM)[None, :]) % FOUR_DIM) / FOUR_DIM
    ch = np.concatenate([np.cos(angc), np.sin(angc)], axis=0) * scale
    f32 = lambda a: jnp.asarray(a, F32)
    return (f32(fa).astype(BF16), f32(cw), f32(sw), f32(fb).astype(BF16), f32(ch).astype(BF16))


def _dft_a_kernel(n1, n_t2, fa_ref, cw_ref, sw_ref, u_ref, o_ref):
    v = jnp.dot(fa_ref[...], u_ref[0], preferred_element_type=F32)
    for j in range(n_t2):
        cw, sw = cw_ref[j], sw_ref[j]
        for g in range(N_FOUR):
            cols = slice(j * N_FOUR * FOUR_DIM + g * FOUR_DIM, j * N_FOUR * FOUR_DIM + (g + 1) * FOUR_DIM)
            vr, vi = v[:n1, cols], v[n1:, cols]
            o_ref[0, :n1, cols] = (vr * cw + vi * sw).astype(o_ref.dtype)
            o_ref[0, n1:, cols] = (vi * cw - vr * sw).astype(o_ref.dtype)


def _dft_b_kernel(n_k1, fb_ref, ch_ref, re_ref, im_ref, o_ref):
    n2 = DFT_N2
    d = N_FOUR * FOUR_DIM
    for j in range(n_k1):
        stack = jnp.concatenate([re_ref[0, j], im_ref[0, j]], axis=0)
        x = jnp.dot(fb_ref[...], stack, preferred_element_type=F32)
        for g in range(N_FOUR):
            cols = slice(g * FOUR_DIM, (g + 1) * FOUR_DIM)
            xg = jnp.concatenate([x[:n2, cols], x[n2:, cols]], axis=1).astype(BF16)
            y = jnp.dot(xg, ch_ref[...], preferred_element_type=F32)
            o_ref[0, :, j * d + g * FOUR_DIM:j * d + (g + 1) * FOUR_DIM] = y.astype(o_ref.dtype)


def _fourier(u):
    B, L, D = u.shape
    n1, n2 = L // DFT_N2, DFT_N2
    fa, cw, sw, fb, ch = _dft_tables(L)
    n_t2 = 8
    const2 = lambda b, i: (0, 0)
    v = pl.pallas_call(
        functools.partial(_dft_a_kernel, n1, n_t2),
        grid=(B, n2 // n_t2),
        in_specs=[pl.BlockSpec(fa.shape, const2),
                  pl.BlockSpec((n_t2, n1, 128), lambda b, i: (i, 0, 0)),
                  pl.BlockSpec((n_t2, n1, 128), lambda b, i: (i, 0, 0)),
                  pl.BlockSpec((1, n1, n_t2 * D), lambda b, i: (b, 0, i))],
        out_specs=pl.BlockSpec((1, 2 * n1, n_t2 * D), lambda b, i: (b, 0, i)),
        out_shape=jax.ShapeDtypeStruct((B, 2 * n1, n2 * D), BF16),
        compiler_params=_params("arbitrary", "arbitrary"),
        name="dft_a",
    )(fa, cw, sw, u.reshape(B, n1, n2 * D))
    v = v.reshape(B, 2 * n1, n2, D)
    n_k1 = min(8, n1)
    nb = n1 // n_k1
    y = pl.pallas_call(
        functools.partial(_dft_b_kernel, n_k1),
        grid=(B, nb),
        in_specs=[pl.BlockSpec(fb.shape, const2),
                  pl.BlockSpec(ch.shape, const2),
                  pl.BlockSpec((1, n_k1, n2, D), lambda b, i: (b, i, 0, 0)),
                  pl.BlockSpec((1, n_k1, n2, D), lambda b, i: (b, nb + i, 0, 0))],
        out_specs=pl.BlockSpec((1, n2, n_k1 * D), lambda b, i: (b, 0, i)),
        out_shape=jax.ShapeDtypeStruct((B, n2, n1 * D), BF16),
        compiler_params=_params("arbitrary", "arbitrary"),
        name="dft_b",
    )(fb, ch, v, v)
    return y.reshape(B, L, D)


def _outproj_kernel(x_ref, of_ref, ob_ref, sg_ref, yf_ref, gt_ref, gn_ref, w_ref, o_ref):
    o = of_ref[0].astype(F32) + ob_ref[0].astype(F32)
    parts = []
    for h in range(N_HEADS):
        cols = slice(h * HEAD, (h + 1) * HEAD)
        parts.append(_rms(o[:, cols], gn_ref[...]))
    o = jnp.concatenate(parts, axis=1) * sg_ref[0].astype(F32)
    mix = jnp.concatenate([o.astype(BF16), yf_ref[0]], axis=1)
    y = jnp.dot(mix, w_ref[...], preferred_element_type=F32)
    o_ref[0] = x_ref[0] + gt_ref[0] * y


def _outproj(x, of, ob, sg, yf, gt1, gn, w_out, tm):
    B, L, D = x.shape
    tm = min(tm, L)
    tok = lambda b, i: (b, i, 0)
    half = pl.BlockSpec((1, tm, D_HGRN), tok)
    return pl.pallas_call(
        _outproj_kernel,
        grid=(B, L // tm),
        in_specs=[pl.BlockSpec((1, tm, D), tok), half, half, half, half,
                  pl.BlockSpec((1, 1, D), lambda b, i: (b, 0, 0)),
                  pl.BlockSpec((1, HEAD), lambda b, i: (0, 0)),
                  pl.BlockSpec(w_out.shape, lambda b, i: (0, 0))],
        out_specs=pl.BlockSpec((1, tm, D), tok),
        out_shape=jax.ShapeDtypeStruct((B, L, D), F32),
        compiler_params=_params("arbitrary", "arbitrary"),
        name="outproj",
    )(x, of, ob, sg, yf, gt1, gn, w_out)


def _gelu_tanh(a):
    return 0.5 * a * (1.0 + jnp.tanh(0.7978845608028654 * (a + 0.044715 * (a * a * a))))


def _ffn_kernel(tm, nt, tf, x_ref, top_ref, bot_ref, sh_ref, sc_ref, gt_ref, g2_ref, gf_ref,
                wa_ref, wu_ref, wdw_ref, bdw_ref, wd_ref, o_ref, h_ref):
    i = pl.program_id(1)
    W = GRID_W

    def prep(xf):
        return _rms(xf, g2_ref[...]) * (1.0 + sc_ref[0]) + sh_ref[0]

    h_ref[0:W, :] = jnp.where(i == 0, 0.0, prep(top_ref[0])).astype(BF16)
    h_ref[W:W + tm, :] = prep(x_ref[0]).astype(BF16)
    h_ref[W + tm:, :] = jnp.where(i == nt - 1, 0.0, prep(bot_ref[0])).astype(BF16)

    colx = lax.broadcasted_iota(jnp.int32, (W, 1), 0)
    first_col = colx == 0
    last_col = colx == W - 1
    nf = wd_ref.shape[0] // tf

    def gate_tile(f):
        cols = slice(f * tf, (f + 1) * tf)
        a = jnp.dot(h_ref[...], wa_ref[:, cols], preferred_element_type=F32)
        u = jnp.dot(h_ref[W:W + tm, :], wu_ref[:, cols], preferred_element_type=F32)
        row_blocks = []
        for r in range(0, tm, W):
            lane_blocks = []
            for l in range(0, tf, 128):
                lanes = slice(l, l + 128)
                taps = [a[r + dr * W:r + (dr + 1) * W, lanes] for dr in range(3)]
                w = lambda k: wdw_ref[k:k + 1, f * tf + l:f * tf + l + 128]
                c = [w(j) * taps[0] + w(3 + j) * taps[1] + w(6 + j) * taps[2] for j in range(3)]
                conv = (jnp.where(first_col, 0.0, pltpu.roll(c[0], 1, axis=0)) + c[1]
                        + jnp.where(last_col, 0.0, pltpu.roll(c[2], W - 1, axis=0))
                        + bdw_ref[:, f * tf + l:f * tf + l + 128])
                lane_blocks.append((_gelu_tanh(conv) * u[r:r + W, lanes]).astype(BF16))
            row_blocks.append(jnp.concatenate(lane_blocks, axis=1))
        return jnp.concatenate(row_blocks, axis=0)

    acc = None
    for f0 in range(0, nf, 2):
        tiles = [gate_tile(f) for f in range(f0, min(f0 + 2, nf))]
        g = tiles[0] if len(tiles) == 1 else jnp.concatenate(tiles, axis=1)
        d = jnp.dot(g, wd_ref[f0 * tf:(f0 + len(tiles)) * tf, :], preferred_element_type=F32)
        acc = d if acc is None else acc + d
    x2 = x_ref[0] + gt_ref[0] * acc
    o_ref[0] = _rms(x2, gf_ref[...])


def _ffn(x, sh, sc, gt, g2, gf, w_up_a, w_up_u, w_dw, b_dw, w_down, tm, tf):
    B, L, D = x.shape
    d_ff = w_down.shape[0]
    tm = min(tm, L)
    nt = L // tm
    rpt = tm // GRID_W
    n_rows = L // GRID_W
    tok = lambda b, i: (b, i, 0)
    vec = lambda b, i: (b, 0, 0)
    nf = d_ff // tf
    resident = lambda a: pl.BlockSpec(a.shape, lambda b, i: (0, 0), pipeline_mode=pl.Buffered(1))
    return pl.pallas_call(
        functools.partial(_ffn_kernel, tm, nt, tf),
        grid=(B, nt),
        in_specs=[pl.BlockSpec((1, tm, D), tok),
                  pl.BlockSpec((1, GRID_W, D), lambda b, i: (b, jnp.maximum(i * rpt - 1, 0), 0)),
                  pl.BlockSpec((1, GRID_W, D), lambda b, i: (b, jnp.minimum((i + 1) * rpt, n_rows - 1), 0)),
                  pl.BlockSpec((1, 1, D), vec), pl.BlockSpec((1, 1, D), vec), pl.BlockSpec((1, 1, D), vec),
                  pl.BlockSpec((1, D), lambda b, i: (0, 0)),
                  pl.BlockSpec((1, D), lambda b, i: (0, 0)),
                  resident(w_up_a), resident(w_up_u), resident(w_dw), resident(b_dw), resident(w_down)],
        out_specs=pl.BlockSpec((1, tm, D), tok),
        out_shape=jax.ShapeDtypeStruct((B, L, D), F32),
        scratch_shapes=[pltpu.VMEM((tm + 2 * GRID_W, D), BF16)],
        compiler_params=_params("arbitrary", "arbitrary"),
        name="ffn",
    )(x, x, x, sh, sc, gt, g2, gf, w_up_a, w_up_u, w_dw, b_dw, w_down)


def kernel(x, c, ctx, c_ctx, w_mod, b_mod, norm1, norm2, w_in, lb_fwd, lb_bwd, hgrn_norm,
           w_out, w_up, w_dw, b_dw, w_down, norm_f):
    B, L, D = x.shape
    depth = w_mod.shape[0]
    assert depth == 1, "single-layer block"
    l = 0
    d_ff = w_down.shape[1]

    n_mod = -(-(B + 1) // 8) * 8
    cc = jnp.zeros((n_mod, D), F32).at[:B].set(c).at[B].set(c_ctx)
    mod = _modulation(cc, w_mod[l], b_mod[l][None, :])
    sh1, sc1, gt1, sh2, sc2, gt2 = [mod[:, j * D:(j + 1) * D][:, None, :] for j in range(6)]
    lat = lambda m: m[:B]
    ctxrow = lambda m: jnp.broadcast_to(m[B:B + 1], (B, 1, D))

    lb = _lower_bounds(lb_fwd, lb_bwd, l)
    w_in_b = w_in[l].astype(BF16)
    g1 = norm1[l][None, :]

    kf_c, lff_c, kb_c, lfb_c, v_c = _inproj(ctx, ctxrow(sh1), ctxrow(sc1), g1, lb, w_in_b,
                                             (_G_FF, _G_FB, _G_V), tm=256)
    zero = jnp.zeros((B, N_HEADS, HEAD, HEAD), F32)
    s0f, s0b = _gla(None, kf_c, lff_c, kb_c, lfb_c, v_c, zero, zero, need_o=False, tl=256)

    q, kf, lff, kb, lfb, v, sg, u = _inproj(x, lat(sh1), lat(sc1), g1, lb, w_in_b,
                                            (_G_Q, _G_FF, _G_FB, _G_V, _G_G, _G_U), tm=512)
    of, ob, _, _ = _gla(q, kf, lff, kb, lfb, v, s0f, s0b, need_o=True, tl=256)
    yf = _fourier(u)
    x1 = _outproj(x, of, ob, sg, yf, lat(gt1), hgrn_norm[l][None, :], w_out[l].astype(BF16), tm=512)

    w_up_b = w_up[l].astype(BF16)
    return _ffn(x1, lat(sh2), lat(sc2), lat(gt2), norm2[l][None, :], norm_f[None, :],
                w_up_b[:, :d_ff], w_up_b[:, d_ff:], w_dw[l].reshape(9, d_ff), b_dw[l][None, :],
                w_down[l].astype(BF16), tm=1024, tf=256)
```

```python
import functools
import math

import jax
import jax.numpy as jnp
import numpy as np
from jax import lax
from jax.experimental import pallas as pl
from jax.experimental.pallas import tpu as pltpu

EPS = 1e-6
GRID_W = 64
CHUNK = 64
HEAD = 128
N_HEADS = 4
D_HGRN = N_HEADS * HEAD
FOUR_DIM = 128
N_FOUR = 4
DFT_N2 = 64
VMEM_LIMIT = 56 * 1024 * 1024

BF16 = jnp.bfloat16
F32 = jnp.float32


def _params(*sem, flags=None):
    return pltpu.CompilerParams(dimension_semantics=sem, vmem_limit_bytes=VMEM_LIMIT, flags=flags)


def _silu(z):
    return z * (1.0 / (1.0 + jnp.exp(-z)))


def _rms(xf, g):
    ms = jnp.mean(xf * xf, axis=-1, keepdims=True)
    return xf * lax.rsqrt(ms + EPS) * g


def _mod_kernel(c_ref, w_ref, b_ref, o_ref):
    s = _silu(c_ref[...])
    o_ref[...] = jnp.dot(s, w_ref[...], preferred_element_type=F32,
                         precision=lax.Precision.HIGHEST) + b_ref[...]


def _modulation(cc, w_mod, b_mod):
    n, d = cc.shape
    n_out = w_mod.shape[1]
    tn = 1024
    return pl.pallas_call(
        _mod_kernel,
        grid=(n_out // tn,),
        in_specs=[pl.BlockSpec((n, d), lambda j: (0, 0)),
                  pl.BlockSpec((d, tn), lambda j: (0, j)),
                  pl.BlockSpec((1, tn), lambda j: (0, j))],
        out_specs=pl.BlockSpec((n, tn), lambda j: (0, j)),
        out_shape=jax.ShapeDtypeStruct((n, n_out), F32),
        compiler_params=_params("arbitrary"),
        name="mod",
    )(cc, w_mod, b_mod)


def _lb_kernel(layer, f_ref, b_ref, o_ref):
    def table(ref):
        t = ref[...]
        e = jnp.exp(t - jnp.max(t, axis=0, keepdims=True))
        p = e / jnp.sum(e, axis=0, keepdims=True)
        return jnp.sum(p[:layer + 1], axis=0, keepdims=True)
    o_ref[0:1, :] = table(f_ref)
    o_ref[1:2, :] = table(b_ref)


def _lower_bounds(lb_fwd, lb_bwd, layer):
    return pl.pallas_call(
        functools.partial(_lb_kernel, layer),
        out_shape=jax.ShapeDtypeStruct((2, lb_fwd.shape[1]), F32),
        name="lb",
    )(lb_fwd, lb_bwd)


_G_Q, _G_FF, _G_FB, _G_V, _G_G, _G_U = range(6)


def _inproj_kernel(groups, x_ref, sh_ref, sc_ref, g_ref, lb_ref, w_ref, *out_refs):
    xf = x_ref[0]
    h = (_rms(xf, g_ref[...]) * (1.0 + sc_ref[0]) + sh_ref[0]).astype(BF16)
    outs = iter(out_refs)
    for grp in groups:
        z = jnp.dot(h, w_ref[:, grp * D_HGRN:(grp + 1) * D_HGRN], preferred_element_type=F32)
        if grp in (_G_Q, _G_G):
            o = next(outs)
            o[0] = _silu(z).astype(o.dtype)
        elif grp in (_G_FF, _G_FB):
            lb = lb_ref[0:1, :] if grp == _G_FF else lb_ref[1:2, :]
            f = lb + (1.0 - lb) * (1.0 / (1.0 + jnp.exp(-z)))
            ko, lo = next(outs), next(outs)
            ko[0] = (1.0 - f).astype(ko.dtype)
            lo[0] = jnp.log(f)
        else:
            o = next(outs)
            o[0] = z.astype(o.dtype)


def _inproj(x, sh, sc, g, lb, w_in, groups, tm):
    B, L, D = x.shape
    tm = min(tm, L)
    out_shape, out_specs = [], []
    tok = lambda b, i: (b, i, 0)
    for grp in groups:
        dts = (BF16, F32) if grp in (_G_FF, _G_FB) else (BF16,)
        for dt in dts:
            out_shape.append(jax.ShapeDtypeStruct((B, L, D_HGRN), dt))
            out_specs.append(pl.BlockSpec((1, tm, D_HGRN), tok))
    return pl.pallas_call(
        functools.partial(_inproj_kernel, groups),
        grid=(B, L // tm),
        in_specs=[pl.BlockSpec((1, tm, D), tok),
                  pl.BlockSpec((1, 1, D), lambda b, i: (b, 0, 0)),
                  pl.BlockSpec((1, 1, D), lambda b, i: (b, 0, 0)),
                  pl.BlockSpec((1, D), lambda b, i: (0, 0)),
                  pl.BlockSpec(lb.shape, lambda b, i: (0, 0)),
                  pl.BlockSpec(w_in.shape, lambda b, i: (0, 0))],
        out_specs=out_specs,
        out_shape=out_shape,
        compiler_params=_params("arbitrary", "arbitrary"),
        name="inproj",
    )(x, sh, sc, g, lb, w_in)


def _gla_direction(rev, need_o, nc, q_ref, k_ref, lf_ref, v_ref, st_ref, o_ref, scr):
    cum_s, qt_s, kt_s, qi_s, kd_s, p_s, u_s, sp_s = scr
    tl = nc * CHUNK
    nt_dims = (((1,), (1,)), ((), ()))
    order = range(nc - 1, -1, -1) if rev else range(nc)
    i_ref = CHUNK - 1 - CHUNK // 2 if rev else CHUNK // 2
    i_last = 0 if rev else CHUNK - 1
    pieces = [(c, h, slice(c * CHUNK, (c + 1) * CHUNK), slice(h * HEAD, (h + 1) * HEAD))
              for c in order for h in range(N_HEADS)]

    r = lax.broadcasted_iota(jnp.int32, (tl, tl), 0)
    c_ = lax.broadcasted_iota(jnp.int32, (tl, tl), 1)
    causal = (r <= c_) if rev else (r >= c_)
    tri = (causal & ((r // CHUNK) == (c_ // CHUNK))).astype(BF16)
    lf = lf_ref[0]
    hi = lf.astype(BF16)
    lo = (lf - hi.astype(F32)).astype(BF16)
    cum_s[...] = (jnp.dot(tri, hi, preferred_element_type=F32)
                  + jnp.dot(tri, lo, preferred_element_type=F32))

    yield
    decay = {}
    for c, h, rows, cols in pieces:
        bh = cum_s[rows, cols]
        b_ref = bh[i_ref:i_ref + 1]
        b_last = bh[i_last:i_last + 1]
        kh = k_ref[0, rows, cols].astype(F32) * jnp.exp(b_ref - bh)
        kd_s[rows, cols] = (kh * jnp.exp(b_last - b_ref)).astype(BF16)
        decay[c, h] = jnp.exp(b_last)
        if need_o:
            kt_s[rows, cols] = kh.astype(BF16)
            qh = q_ref[0, rows, cols].astype(F32) * jnp.exp(bh - b_ref)
            qt_s[rows, cols] = qh.astype(BF16)
            qi_s[rows, cols] = (qh * jnp.exp(b_ref)).astype(BF16)

    yield
    keep = causal[:CHUNK, :CHUNK]
    for c, h, rows, cols in pieces:
        u_s[c, h] = lax.dot_general(v_ref[0, rows, cols], kd_s[rows, cols], (((0,), (0,)), ((), ())),
                                    preferred_element_type=F32)
        if need_o:
            s = lax.dot_general(qt_s[rows, cols], kt_s[rows, cols], nt_dims, preferred_element_type=F32)
            p_s[rows, h * CHUNK:(h + 1) * CHUNK] = jnp.where(keep, s, 0.0).astype(BF16)

    yield
    for h in range(N_HEADS):
        st = st_ref[0, h]
        for c in order:
            if need_o:
                sp_s[c, h] = st.astype(BF16)
            st = decay[c, h] * st + u_s[c, h]
        st_ref[0, h] = st

    yield
    if need_o:
        for c, h, rows, cols in pieces:
            o = jnp.dot(p_s[rows, h * CHUNK:(h + 1) * CHUNK], v_ref[0, rows, cols], preferred_element_type=F32)
            o = o + lax.dot_general(qi_s[rows, cols], sp_s[c, h], nt_dims, preferred_element_type=F32)
            o_ref[0, rows, cols] = o.astype(o_ref.dtype)
    yield


def _gla_scratch(tl):
    nc = tl // CHUNK
    wide = lambda dt: pltpu.VMEM((tl, D_HGRN), dt)
    return [wide(F32), wide(BF16), wide(BF16), wide(BF16), wide(BF16),
            pltpu.VMEM((tl, N_HEADS * CHUNK), BF16),
            pltpu.VMEM((nc, N_HEADS, HEAD, HEAD), F32),
            pltpu.VMEM((nc, N_HEADS, HEAD, HEAD), BF16)]


def _gla_kernel(need_o, nc, *refs):
    n_scr = len(_gla_scratch(CHUNK))
    scr_f, scr_b = refs[-2 * n_scr:-n_scr], refs[-n_scr:]
    refs = refs[:-2 * n_scr]
    if need_o:
        (qf, kf, lff, vf, qb, kb, lfb, vb, s0f, s0b, of, ob, sf, sb) = refs
    else:
        (kf, lff, vf, kb, lfb, vb, s0f, s0b, sf, sb) = refs
        qf = qb = of = ob = None

    @pl.when(pl.program_id(1) == 0)
    def _():
        sf[...] = s0f[...]
        sb[...] = s0b[...]

    for _ in zip(_gla_direction(False, need_o, nc, qf, kf, lff, vf, sf, of, scr_f),
                 _gla_direction(True, need_o, nc, qb, kb, lfb, vb, sb, ob, scr_b)):
        pass


def _gla(q, kf, lff, kb, lfb, v, s0f, s0b, need_o, tl):
    B, L, _ = v.shape
    tl = min(tl, L)
    nt = L // tl
    fwd = pl.BlockSpec((1, tl, D_HGRN), lambda b, i: (b, i, 0))
    bwd = pl.BlockSpec((1, tl, D_HGRN), lambda b, i: (b, nt - 1 - i, 0))
    st = pl.BlockSpec((1, N_HEADS, HEAD, HEAD), lambda b, i: (b, 0, 0, 0))
    st_shape = jax.ShapeDtypeStruct((B, N_HEADS, HEAD, HEAD), F32)
    if need_o:
        ins = (q, kf, lff, v, q, kb, lfb, v, s0f, s0b)
        in_specs = [fwd] * 4 + [bwd] * 4 + [st, st]
        out_specs = [fwd, bwd, st, st]
        o_shape = jax.ShapeDtypeStruct((B, L, D_HGRN), BF16)
        out_shape = [o_shape, o_shape, st_shape, st_shape]
    else:
        ins = (kf, lff, v, kb, lfb, v, s0f, s0b)
        in_specs = [fwd] * 3 + [bwd] * 3 + [st, st]
        out_specs = [st, st]
        out_shape = [st_shape, st_shape]
    return pl.pallas_call(
        functools.partial(_gla_kernel, need_o, tl // CHUNK),
        grid=(B, nt),
        in_specs=in_specs,
        out_specs=out_specs,
        out_shape=out_shape,
        scratch_shapes=_gla_scratch(tl) + _gla_scratch(tl),
        compiler_params=_params("arbitrary", "arbitrary"),
        name="gla",
    )(*ins)


DFT_NJ = 16


def _dft_tables(L):
    n1, n2, nj = L // DFT_N2, DFT_N2, DFT_NJ
    scale = 1.0 / math.sqrt(L * FOUR_DIM)
    k1 = np.arange(n1)[:, None]
    t1 = np.arange(n1)[None, :]
    ang = 2.0 * np.pi * ((k1 * t1) % n1) / n1
    fa = np.concatenate([np.cos(ang), -np.sin(ang)], axis=0)
    ga = np.kron(fa, np.eye(nj))
    t2 = np.arange(n2).reshape(n2 // nj, 1, nj)
    angw = 2.0 * np.pi * ((t2 * np.arange(n1)[None, :, None]) % L) / L
    lanes = lambda a: np.broadcast_to(a.reshape(n2 // nj, n1 * nj, 1), (n2 // nj, n1 * nj, 128))
    cw, sw = lanes(np.cos(angw)), lanes(np.sin(angw))
    k2 = np.arange(n2)[:, None]
    ang2 = 2.0 * np.pi * ((k2 * np.arange(n2)[None, :]) % n2) / n2
    c2, s2 = np.cos(ang2), np.sin(ang2)
    fb = np.block([[c2, s2], [-s2, c2]])
    n = np.arange(FOUR_DIM)[:, None]
    angc = 2.0 * np.pi * ((n * np.arange(FOUR_DIM)[None, :]) % FOUR_DIM) / FOUR_DIM
    ch = np.concatenate([np.cos(angc), np.sin(angc)], axis=0) * scale
    f32 = lambda a: jnp.asarray(a, F32)
    return (f32(ga).astype(BF16), f32(cw), f32(sw), f32(fb).astype(BF16), f32(ch).astype(BF16))


def _dft_a_kernel(n1, ga_ref, cw_ref, sw_ref, u_ref, o_ref):
    nj = DFT_NJ
    half = n1 * nj
    rows = u_ref[0].reshape(half, u_ref.shape[-1])
    v = jnp.dot(ga_ref[...], rows, preferred_element_type=F32)
    cw, sw = cw_ref[0], sw_ref[0]
    for g in range(N_FOUR):
        cols = slice(g * FOUR_DIM, (g + 1) * FOUR_DIM)
        vr, vi = v[:half, cols], v[half:, cols]
        o_ref[0, :n1, :, cols] = (vr * cw + vi * sw).reshape(n1, nj, FOUR_DIM).astype(o_ref.dtype)
        o_ref[0, n1:, :, cols] = (vi * cw - vr * sw).reshape(n1, nj, FOUR_DIM).astype(o_ref.dtype)


def _dft_b_kernel(n_k1, fb_ref, ch_ref, re_ref, im_ref, o_ref):
    n2 = DFT_N2
    pieces = []
    for j in range(n_k1):
        stack = jnp.concatenate([re_ref[0, j], im_ref[0, j]], axis=0)
        x = jnp.dot(fb_ref[...], stack, preferred_element_type=F32)
        for g in range(N_FOUR):
            cols = slice(g * FOUR_DIM, (g + 1) * FOUR_DIM)
            pieces.append(jnp.concatenate([x[:n2, cols], x[n2:, cols]], axis=1).astype(BF16))
    y = jnp.dot(jnp.concatenate(pieces, axis=0), ch_ref[...], preferred_element_type=F32)
    for p in range(n_k1 * N_FOUR):
        o_ref[0, :, p * FOUR_DIM:(p + 1) * FOUR_DIM] = y[p * n2:(p + 1) * n2].astype(o_ref.dtype)


def _fourier(u):
    B, L, D = u.shape
    n1, n2, nj = L // DFT_N2, DFT_N2, DFT_NJ
    ga, cw, sw, fb, ch = _dft_tables(L)
    const2 = lambda b, i: (0, 0)
    v = pl.pallas_call(
        functools.partial(_dft_a_kernel, n1),
        grid=(B, n2 // nj),
        in_specs=[pl.BlockSpec(ga.shape, const2),
                  pl.BlockSpec((1, n1 * nj, 128), lambda b, i: (i, 0, 0)),
                  pl.BlockSpec((1, n1 * nj, 128), lambda b, i: (i, 0, 0)),
                  pl.BlockSpec((1, n1, nj, D), lambda b, i: (b, 0, i, 0))],
        out_specs=pl.BlockSpec((1, 2 * n1, nj, D), lambda b, i: (b, 0, i, 0)),
        out_shape=jax.ShapeDtypeStruct((B, 2 * n1, n2, D), BF16),
        compiler_params=_params("arbitrary", "arbitrary"),
        name="dft_a",
    )(ga, cw, sw, u.reshape(B, n1, n2, D))
    n_k1 = min(8, n1)
    nb = n1 // n_k1
    y = pl.pallas_call(
        functools.partial(_dft_b_kernel, n_k1),
        grid=(B, nb),
        in_specs=[pl.BlockSpec(fb.shape, const2),
                  pl.BlockSpec(ch.shape, const2),
                  pl.BlockSpec((1, n_k1, n2, D), lambda b, i: (b, i, 0, 0)),
                  pl.BlockSpec((1, n_k1, n2, D), lambda b, i: (b, nb + i, 0, 0))],
        out_specs=pl.BlockSpec((1, n2, n_k1 * D), lambda b, i: (b, 0, i)),
        out_shape=jax.ShapeDtypeStruct((B, n2, n1 * D), BF16),
        compiler_params=_params("arbitrary", "arbitrary"),
        name="dft_b",
    )(fb, ch, v, v)
    return y.reshape(B, L, D)


def _outproj_kernel(x_ref, of_ref, ob_ref, sg_ref, yf_ref, gt_ref, gn_ref, w_ref, o_ref):
    o = of_ref[0].astype(F32) + ob_ref[0].astype(F32)
    parts = []
    for h in range(N_HEADS):
        cols = slice(h * HEAD, (h + 1) * HEAD)
        parts.append(_rms(o[:, cols], gn_ref[...]))
    o = jnp.concatenate(parts, axis=1) * sg_ref[0].astype(F32)
    mix = jnp.concatenate([o.astype(BF16), yf_ref[0]], axis=1)
    y = jnp.dot(mix, w_ref[...], preferred_element_type=F32)
    o_ref[0] = x_ref[0] + gt_ref[0] * y


def _outproj(x, of, ob, sg, yf, gt1, gn, w_out, tm):
    B, L, D = x.shape
    tm = min(tm, L)
    tok = lambda b, i: (b, i, 0)
    half = pl.BlockSpec((1, tm, D_HGRN), tok)
    return pl.pallas_call(
        _outproj_kernel,
        grid=(B, L // tm),
        in_specs=[pl.BlockSpec((1, tm, D), tok), half, half, half, half,
                  pl.BlockSpec((1, 1, D), lambda b, i: (b, 0, 0)),
                  pl.BlockSpec((1, HEAD), lambda b, i: (0, 0)),
                  pl.BlockSpec(w_out.shape, lambda b, i: (0, 0))],
        out_specs=pl.BlockSpec((1, tm, D), tok),
        out_shape=jax.ShapeDtypeStruct((B, L, D), F32),
        compiler_params=_params("arbitrary", "arbitrary"),
        name="outproj",
    )(x, of, ob, sg, yf, gt1, gn, w_out)


def _gelu_tanh(a):
    return 0.5 * a * (1.0 + jnp.tanh(0.7978845608028654 * (a + 0.044715 * (a * a * a))))


def _ffn_kernel(tm, nt, tf, x_ref, top_ref, bot_ref, sh_ref, sc_ref, gt_ref, g2_ref, gf_ref,
                wa_ref, wu_ref, wdw_ref, bdw_ref, wd_ref, o_ref, h_ref):
    i = pl.program_id(1)
    W = GRID_W

    def prep(xf):
        return _rms(xf, g2_ref[...]) * (1.0 + sc_ref[0]) + sh_ref[0]

    h_ref[0:W, :] = jnp.where(i == 0, 0.0, prep(top_ref[0])).astype(BF16)
    h_ref[W:W + tm, :] = prep(x_ref[0]).astype(BF16)
    h_ref[W + tm:, :] = jnp.where(i == nt - 1, 0.0, prep(bot_ref[0])).astype(BF16)

    colx = lax.broadcasted_iota(jnp.int32, (W, 1), 0)
    first_col = colx == 0
    last_col = colx == W - 1
    nf = wd_ref.shape[0] // tf

    def gate_tile(f):
        cols = slice(f * tf, (f + 1) * tf)
        a = jnp.dot(h_ref[...], wa_ref[:, cols], preferred_element_type=F32)
        u = jnp.dot(h_ref[W:W + tm, :], wu_ref[:, cols], preferred_element_type=F32)
        row_blocks = []
        for r in range(0, tm, W):
            lane_blocks = []
            for l in range(0, tf, 128):
                lanes = slice(l, l + 128)
                taps = [a[r + dr * W:r + (dr + 1) * W, lanes] for dr in range(3)]
                w = lambda k: wdw_ref[k:k + 1, f * tf + l:f * tf + l + 128]
                c = [w(j) * taps[0] + w(3 + j) * taps[1] + w(6 + j) * taps[2] for j in range(3)]
                conv = (jnp.where(first_col, 0.0, pltpu.roll(c[0], 1, axis=0)) + c[1]
                        + jnp.where(last_col, 0.0, pltpu.roll(c[2], W - 1, axis=0))
                        + bdw_ref[:, f * tf + l:f * tf + l + 128])
                lane_blocks.append((_gelu_tanh(conv) * u[r:r + W, lanes]).astype(BF16))
            row_blocks.append(jnp.concatenate(lane_blocks, axis=1))
        return jnp.concatenate(row_blocks, axis=0)

    acc = None
    for f0 in range(0, nf, 2):
        tiles = [gate_tile(f) for f in range(f0, min(f0 + 2, nf))]
        g = tiles[0] if len(tiles) == 1 else jnp.concatenate(tiles, axis=1)
        d = jnp.dot(g, wd_ref[f0 * tf:(f0 + len(tiles)) * tf, :], preferred_element_type=F32)
        acc = d if acc is None else acc + d
    x2 = x_ref[0] + gt_ref[0] * acc
    o_ref[0] = _rms(x2, gf_ref[...])


def _ffn(x, sh, sc, gt, g2, gf, w_up_a, w_up_u, w_dw, b_dw, w_down, tm, tf):
    B, L, D = x.shape
    d_ff = w_down.shape[0]
    tm = min(tm, L)
    nt = L // tm
    rpt = tm // GRID_W
    n_rows = L // GRID_W
    tok = lambda b, i: (b, i, 0)
    vec = lambda b, i: (b, 0, 0)
    nf = d_ff // tf
    resident = lambda a: pl.BlockSpec(a.shape, lambda b, i: (0, 0), pipeline_mode=pl.Buffered(1))
    return pl.pallas_call(
        functools.partial(_ffn_kernel, tm, nt, tf),
        grid=(B, nt),
        in_specs=[pl.BlockSpec((1, tm, D), tok),
                  pl.BlockSpec((1, GRID_W, D), lambda b, i: (b, jnp.maximum(i * rpt - 1, 0), 0)),
                  pl.BlockSpec((1, GRID_W, D), lambda b, i: (b, jnp.minimum((i + 1) * rpt, n_rows - 1), 0)),
                  pl.BlockSpec((1, 1, D), vec), pl.BlockSpec((1, 1, D), vec), pl.BlockSpec((1, 1, D), vec),
                  pl.BlockSpec((1, D), lambda b, i: (0, 0)),
                  pl.BlockSpec((1, D), lambda b, i: (0, 0)),
                  resident(w_up_a), resident(w_up_u), resident(w_dw), resident(b_dw), resident(w_down)],
        out_specs=pl.BlockSpec((1, tm, D), tok),
        out_shape=jax.ShapeDtypeStruct((B, L, D), F32),
        scratch_shapes=[pltpu.VMEM((tm + 2 * GRID_W, D), BF16)],
        compiler_params=_params("arbitrary", "arbitrary"),
        name="ffn",
    )(x, x, x, sh, sc, gt, g2, gf, w_up_a, w_up_u, w_dw, b_dw, w_down)


def kernel(x, c, ctx, c_ctx, w_mod, b_mod, norm1, norm2, w_in, lb_fwd, lb_bwd, hgrn_norm,
           w_out, w_up, w_dw, b_dw, w_down, norm_f):
    B, L, D = x.shape
    depth = w_mod.shape[0]
    assert depth == 1, "single-layer block"
    l = 0
    d_ff = w_down.shape[1]

    n_mod = -(-(B + 1) // 8) * 8
    cc = jnp.zeros((n_mod, D), F32).at[:B].set(c).at[B].set(c_ctx)
    mod = _modulation(cc, w_mod[l], b_mod[l][None, :])
    sh1, sc1, gt1, sh2, sc2, gt2 = [mod[:, j * D:(j + 1) * D][:, None, :] for j in range(6)]
    lat = lambda m: m[:B]
    ctxrow = lambda m: jnp.broadcast_to(m[B:B + 1], (B, 1, D))

    lb = _lower_bounds(lb_fwd, lb_bwd, l)
    w_in_b = w_in[l].astype(BF16)
    g1 = norm1[l][None, :]

    kf_c, lff_c, kb_c, lfb_c, v_c = _inproj(ctx, ctxrow(sh1), ctxrow(sc1), g1, lb, w_in_b,
                                             (_G_FF, _G_FB, _G_V), tm=256)
    zero = jnp.zeros((B, N_HEADS, HEAD, HEAD), F32)
    s0f, s0b = _gla(None, kf_c, lff_c, kb_c, lfb_c, v_c, zero, zero, need_o=False, tl=256)

    q, kf, lff, kb, lfb, v, sg, u = _inproj(x, lat(sh1), lat(sc1), g1, lb, w_in_b,
                                            (_G_Q, _G_FF, _G_FB, _G_V, _G_G, _G_U), tm=512)
    of, ob, _, _ = _gla(q, kf, lff, kb, lfb, v, s0f, s0b, need_o=True, tl=256)
    yf = _fourier(u)
    x1 = _outproj(x, of, ob, sg, yf, lat(gt1), hgrn_norm[l][None, :], w_out[l].astype(BF16), tm=512)

    w_up_b = w_up[l].astype(BF16)
    return _ffn(x1, lat(sh2), lat(sc2), lat(gt2), norm2[l][None, :], norm_f[None, :],
                w_up_b[:, :d_ff], w_up_b[:, d_ff:], w_dw[l].reshape(9, d_ff), b_dw[l][None, :],
                w_down[l].astype(BF16), tm=1024, tf=256)
```

```python
import functools
import math

import jax
import jax.numpy as jnp
import numpy as np
from jax import lax
from jax.experimental import pallas as pl
from jax.experimental.pallas import tpu as pltpu

EPS = 1e-6
GRID_W = 64
CHUNK = 64
HEAD = 128
OUT_K = 256
N_HEADS = 4
D_HGRN = N_HEADS * HEAD
FOUR_DIM = 128
N_FOUR = 4
DFT_N2 = 64
VMEM_LIMIT = 56 * 1024 * 1024

BF16 = jnp.bfloat16
F32 = jnp.float32


def _params(*sem, flags=None):
    return pltpu.CompilerParams(dimension_semantics=sem, vmem_limit_bytes=VMEM_LIMIT, flags=flags)


def _silu(z):
    return z * (1.0 / (1.0 + jnp.exp(-z)))


def _rms(xf, g):
    ms = jnp.mean(xf * xf, axis=-1, keepdims=True)
    return xf * lax.rsqrt(ms + EPS) * g


def _mod_kernel(c_ref, w_ref, b_ref, o_ref):
    s = _silu(c_ref[...])
    o_ref[...] = jnp.dot(s, w_ref[...], preferred_element_type=F32,
                         precision=lax.Precision.HIGHEST) + b_ref[...]


def _modulation(cc, w_mod, b_mod):
    n, d = cc.shape
    n_out = w_mod.shape[1]
    tn = 1024
    return pl.pallas_call(
        _mod_kernel,
        grid=(n_out // tn,),
        in_specs=[pl.BlockSpec((n, d), lambda j: (0, 0)),
                  pl.BlockSpec((d, tn), lambda j: (0, j)),
                  pl.BlockSpec((1, tn), lambda j: (0, j))],
        out_specs=pl.BlockSpec((n, tn), lambda j: (0, j)),
        out_shape=jax.ShapeDtypeStruct((n, n_out), F32),
        compiler_params=_params("arbitrary"),
        name="mod",
    )(cc, w_mod, b_mod)


def _lb_kernel(layer, f_ref, b_ref, o_ref):
    def table(ref):
        t = ref[...]
        e = jnp.exp(t - jnp.max(t, axis=0, keepdims=True))
        p = e / jnp.sum(e, axis=0, keepdims=True)
        return jnp.sum(p[:layer + 1], axis=0, keepdims=True)
    o_ref[0:1, :] = table(f_ref)
    o_ref[1:2, :] = table(b_ref)


def _lower_bounds(lb_fwd, lb_bwd, layer):
    return pl.pallas_call(
        functools.partial(_lb_kernel, layer),
        out_shape=jax.ShapeDtypeStruct((2, lb_fwd.shape[1]), F32),
        name="lb",
    )(lb_fwd, lb_bwd)


_G_Q, _G_FF, _G_FB, _G_V, _G_G, _G_U = range(6)


def _inproj_kernel(groups, x_ref, sh_ref, sc_ref, g_ref, lb_ref, w_ref, *out_refs):
    xf = x_ref[0]
    h = (_rms(xf, g_ref[...]) * (1.0 + sc_ref[0]) + sh_ref[0]).astype(BF16)
    outs = iter(out_refs)
    for grp in groups:
        z = jnp.dot(h, w_ref[:, grp * D_HGRN:(grp + 1) * D_HGRN], preferred_element_type=F32)
        if grp in (_G_Q, _G_G):
            o = next(outs)
            o[0] = _silu(z).astype(o.dtype)
        elif grp in (_G_FF, _G_FB):
            lb = lb_ref[0:1, :] if grp == _G_FF else lb_ref[1:2, :]
            f = lb + (1.0 - lb) * (1.0 / (1.0 + jnp.exp(-z)))
            ko, lo = next(outs), next(outs)
            ko[0] = (1.0 - f).astype(ko.dtype)
            lo[0] = jnp.log(f)
        else:
            o = next(outs)
            o[0] = z.astype(o.dtype)


def _inproj(x, sh, sc, g, lb, w_in, groups, tm):
    B, L, D = x.shape
    tm = min(tm, L)
    out_shape, out_specs = [], []
    tok = lambda b, i: (b, i, 0)
    for grp in groups:
        dts = (BF16, F32) if grp in (_G_FF, _G_FB) else (BF16,)
        for dt in dts:
            out_shape.append(jax.ShapeDtypeStruct((B, L, D_HGRN), dt))
            out_specs.append(pl.BlockSpec((1, tm, D_HGRN), tok))
    return pl.pallas_call(
        functools.partial(_inproj_kernel, groups),
        grid=(B, L // tm),
        in_specs=[pl.BlockSpec((1, tm, D), tok),
                  pl.BlockSpec((1, 1, D), lambda b, i: (b, 0, 0)),
                  pl.BlockSpec((1, 1, D), lambda b, i: (b, 0, 0)),
                  pl.BlockSpec((1, D), lambda b, i: (0, 0)),
                  pl.BlockSpec(lb.shape, lambda b, i: (0, 0)),
                  pl.BlockSpec(w_in.shape, lambda b, i: (0, 0))],
        out_specs=out_specs,
        out_shape=out_shape,
        compiler_params=_params("arbitrary", "arbitrary"),
        name="inproj",
    )(x, sh, sc, g, lb, w_in)


def _gla_direction(rev, need_o, nc, q_ref, k_ref, lf_ref, v_ref, st_ref, o_ref, scr):
    cum_s, qt_s, kt_s, kd_s, u_s, lhs_s, rhs_s = scr
    tl = nc * CHUNK
    nt_dims = (((1,), (1,)), ((), ()))
    order = range(nc - 1, -1, -1) if rev else range(nc)
    i_ref = CHUNK - 1 - CHUNK // 2 if rev else CHUNK // 2
    i_last = 0 if rev else CHUNK - 1
    pieces = [(c, h, slice(c * CHUNK, (c + 1) * CHUNK), slice(h * HEAD, (h + 1) * HEAD))
              for c in order for h in range(N_HEADS)]

    r = lax.broadcasted_iota(jnp.int32, (tl, tl), 0)
    c_ = lax.broadcasted_iota(jnp.int32, (tl, tl), 1)
    causal = (r <= c_) if rev else (r >= c_)
    tri = (causal & ((r // CHUNK) == (c_ // CHUNK))).astype(BF16)
    lf = lf_ref[0]
    hi = lf.astype(BF16)
    lo = (lf - hi.astype(F32)).astype(BF16)
    cum_s[...] = (jnp.dot(tri, hi, preferred_element_type=F32)
                  + jnp.dot(tri, lo, preferred_element_type=F32))

    yield
    decay = {}
    for c, h, rows, cols in pieces:
        bh = cum_s[rows, cols]
        b_ref = bh[i_ref:i_ref + 1]
        b_last = bh[i_last:i_last + 1]
        e_fwd = jnp.exp(bh - b_ref)
        kt = k_ref[0, rows, cols] * (1.0 / e_fwd).astype(BF16)
        kd_s[rows, cols] = kt * jnp.exp(b_last - b_ref).astype(BF16)
        decay[c, h] = jnp.exp(b_last)
        if need_o:
            kt_s[rows, cols] = kt
            qt = q_ref[0, rows, cols] * e_fwd.astype(BF16)
            qt_s[rows, cols] = qt
            lhs_s[rows, h * OUT_K:h * OUT_K + HEAD] = qt * jnp.exp(b_ref).astype(BF16)

    yield
    keep = causal[:CHUNK, :CHUNK]
    for c, h, rows, cols in pieces:
        u_s[c, h] = lax.dot_general(v_ref[0, rows, cols], kd_s[rows, cols], (((0,), (0,)), ((), ())),
                                    preferred_element_type=F32)
        if need_o:
            s = lax.dot_general(qt_s[rows, cols], kt_s[rows, cols], nt_dims, preferred_element_type=F32)
            lhs_s[rows, h * OUT_K + HEAD:h * OUT_K + HEAD + CHUNK] = jnp.where(keep, s, 0.0).astype(BF16)
            rhs_s[c, h, HEAD:HEAD + CHUNK, :] = v_ref[0, rows, cols]

    yield
    for h in range(N_HEADS):
        st = st_ref[0, h]
        for c in order:
            if need_o:
                rhs_s[c, h, 0:HEAD, :] = st.T.astype(BF16)
            st = decay[c, h] * st + u_s[c, h]
        st_ref[0, h] = st

    yield
    if need_o:
        for c, h, rows, cols in pieces:
            o = jnp.dot(lhs_s[rows, h * OUT_K:(h + 1) * OUT_K], rhs_s[c, h], preferred_element_type=F32)
            o_ref[0, rows, cols] = o.astype(o_ref.dtype)
    yield


def _gla_scratch(tl):
    nc = tl // CHUNK
    wide = lambda dt: pltpu.VMEM((tl, D_HGRN), dt)
    return [wide(F32), wide(BF16), wide(BF16), wide(BF16),
            pltpu.VMEM((nc, N_HEADS, HEAD, HEAD), F32),
            pltpu.VMEM((tl, N_HEADS * OUT_K), BF16),
            pltpu.VMEM((nc, N_HEADS, OUT_K, HEAD), BF16)]


def _gla_kernel(need_o, nc, *refs):
    n_scr = len(_gla_scratch(CHUNK))
    scr_f, scr_b = refs[-2 * n_scr:-n_scr], refs[-n_scr:]
    refs = refs[:-2 * n_scr]
    if need_o:
        (qf, kf, lff, vf, qb, kb, lfb, vb, s0f, s0b, of, ob, sf, sb) = refs
    else:
        (kf, lff, vf, kb, lfb, vb, s0f, s0b, sf, sb) = refs
        qf = qb = of = ob = None

    @pl.when(pl.program_id(1) == 0)
    def _():
        sf[...] = s0f[...]
        sb[...] = s0b[...]
        for scr in (scr_f, scr_b):
            lhs_s, rhs_s = scr[-2:]
            for h in range(N_HEADS):
                lhs_s[:, h * OUT_K + HEAD + CHUNK:(h + 1) * OUT_K] = jnp.zeros(
                    (lhs_s.shape[0], OUT_K - HEAD - CHUNK), BF16)
            rhs_s[:, :, HEAD + CHUNK:, :] = jnp.zeros(rhs_s.shape[:2] + (OUT_K - HEAD - CHUNK, HEAD), BF16)

    for _ in zip(_gla_direction(False, need_o, nc, qf, kf, lff, vf, sf, of, scr_f),
                 _gla_direction(True, need_o, nc, qb, kb, lfb, vb, sb, ob, scr_b)):
        pass


def _gla(q, kf, lff, kb, lfb, v, s0f, s0b, need_o, tl):
    B, L, _ = v.shape
    tl = min(tl, L)
    nt = L // tl
    fwd = pl.BlockSpec((1, tl, D_HGRN), lambda b, i: (b, i, 0))
    bwd = pl.BlockSpec((1, tl, D_HGRN), lambda b, i: (b, nt - 1 - i, 0))
    st = pl.BlockSpec((1, N_HEADS, HEAD, HEAD), lambda b, i: (b, 0, 0, 0))
    st_shape = jax.ShapeDtypeStruct((B, N_HEADS, HEAD, HEAD), F32)
    if need_o:
        ins = (q, kf, lff, v, q, kb, lfb, v, s0f, s0b)
        in_specs = [fwd] * 4 + [bwd] * 4 + [st, st]
        out_specs = [fwd, bwd, st, st]
        o_shape = jax.ShapeDtypeStruct((B, L, D_HGRN), BF16)
        out_shape = [o_shape, o_shape, st_shape, st_shape]
    else:
        ins = (kf, lff, v, kb, lfb, v, s0f, s0b)
        in_specs = [fwd] * 3 + [bwd] * 3 + [st, st]
        out_specs = [st, st]
        out_shape = [st_shape, st_shape]
    return pl.pallas_call(
        functools.partial(_gla_kernel, need_o, tl // CHUNK),
        grid=(B, nt),
        in_specs=in_specs,
        out_specs=out_specs,
        out_shape=out_shape,
        scratch_shapes=_gla_scratch(tl) + _gla_scratch(tl),
        compiler_params=_params("arbitrary", "arbitrary"),
        name="gla",
    )(*ins)


DFT_NJ = 16


def _dft_tables(L):
    n1, n2, nj = L // DFT_N2, DFT_N2, DFT_NJ
    scale = 1.0 / math.sqrt(L * FOUR_DIM)
    k1 = np.arange(n1)[:, None]
    t1 = np.arange(n1)[None, :]
    ang = 2.0 * np.pi * ((k1 * t1) % n1) / n1
    fa = np.concatenate([np.cos(ang), -np.sin(ang)], axis=0)
    ga = np.kron(fa, np.eye(nj))
    t2 = np.arange(n2).reshape(n2 // nj, 1, nj)
    angw = 2.0 * np.pi * ((t2 * np.arange(n1)[None, :, None]) % L) / L
    lanes = lambda a: np.broadcast_to(a.reshape(n2 // nj, n1 * nj, 1), (n2 // nj, n1 * nj, 128))
    cw, sw = lanes(np.cos(angw)), lanes(np.sin(angw))
    k2 = np.arange(n2)[:, None]
    ang2 = 2.0 * np.pi * ((k2 * np.arange(n2)[None, :]) % n2) / n2
    c2, s2 = np.cos(ang2), np.sin(ang2)
    fb = np.block([[c2, s2], [-s2, c2]])
    n = np.arange(FOUR_DIM)[:, None]
    angc = 2.0 * np.pi * ((n * np.arange(FOUR_DIM)[None, :]) % FOUR_DIM) / FOUR_DIM
    ch = np.concatenate([np.cos(angc), np.sin(angc)], axis=0) * scale
    f32 = lambda a: jnp.asarray(a, F32)
    return (f32(ga).astype(BF16), f32(cw), f32(sw), f32(fb).astype(BF16), f32(ch).astype(BF16))


def _dft_a_kernel(n1, ga_ref, cw_ref, sw_ref, u_ref, o_ref):
    nj = DFT_NJ
    half = n1 * nj
    rows = u_ref[0].reshape(half, u_ref.shape[-1])
    v = jnp.dot(ga_ref[...], rows, preferred_element_type=F32)
    cw, sw = cw_ref[0], sw_ref[0]
    for g in range(N_FOUR):
        cols = slice(g * FOUR_DIM, (g + 1) * FOUR_DIM)
        vr, vi = v[:half, cols], v[half:, cols]
        o_ref[0, :n1, :, cols] = (vr * cw + vi * sw).reshape(n1, nj, FOUR_DIM).astype(o_ref.dtype)
        o_ref[0, n1:, :, cols] = (vi * cw - vr * sw).reshape(n1, nj, FOUR_DIM).astype(o_ref.dtype)


def _dft_b_kernel(n_k1, fb_ref, ch_ref, re_ref, im_ref, o_ref):
    n2 = DFT_N2
    pieces = []
    for j in range(n_k1):
        stack = jnp.concatenate([re_ref[0, j], im_ref[0, j]], axis=0)
        x = jnp.dot(fb_ref[...], stack, preferred_element_type=F32)
        for g in range(N_FOUR):
            cols = slice(g * FOUR_DIM, (g + 1) * FOUR_DIM)
            pieces.append(jnp.concatenate([x[:n2, cols], x[n2:, cols]], axis=1).astype(BF16))
    y = jnp.dot(jnp.concatenate(pieces, axis=0), ch_ref[...], preferred_element_type=F32)
    for p in range(n_k1 * N_FOUR):
        o_ref[0, :, p * FOUR_DIM:(p + 1) * FOUR_DIM] = y[p * n2:(p + 1) * n2].astype(o_ref.dtype)


def _fourier(u):
    B, L, D = u.shape
    n1, n2, nj = L // DFT_N2, DFT_N2, DFT_NJ
    ga, cw, sw, fb, ch = _dft_tables(L)
    const2 = lambda b, i: (0, 0)
    v = pl.pallas_call(
        functools.partial(_dft_a_kernel, n1),
        grid=(B, n2 // nj),
        in_specs=[pl.BlockSpec(ga.shape, const2),
                  pl.BlockSpec((1, n1 * nj, 128), lambda b, i: (i, 0, 0)),
                  pl.BlockSpec((1, n1 * nj, 128), lambda b, i: (i, 0, 0)),
                  pl.BlockSpec((1, n1, nj, D), lambda b, i: (b, 0, i, 0))],
        out_specs=pl.BlockSpec((1, 2 * n1, nj, D), lambda b, i: (b, 0, i, 0)),
        out_shape=jax.ShapeDtypeStruct((B, 2 * n1, n2, D), BF16),
        compiler_params=_params("arbitrary", "arbitrary"),
        name="dft_a",
    )(ga, cw, sw, u.reshape(B, n1, n2, D))
    n_k1 = min(8, n1)
    nb = n1 // n_k1
    y = pl.pallas_call(
        functools.partial(_dft_b_kernel, n_k1),
        grid=(B, nb),
        in_specs=[pl.BlockSpec(fb.shape, const2),
                  pl.BlockSpec(ch.shape, const2),
                  pl.BlockSpec((1, n_k1, n2, D), lambda b, i: (b, i, 0, 0)),
                  pl.BlockSpec((1, n_k1, n2, D), lambda b, i: (b, nb + i, 0, 0))],
        out_specs=pl.BlockSpec((1, n2, n_k1 * D), lambda b, i: (b, 0, i)),
        out_shape=jax.ShapeDtypeStruct((B, n2, n1 * D), BF16),
        compiler_params=_params("arbitrary", "arbitrary"),
        name="dft_b",
    )(fb, ch, v, v)
    return y.reshape(B, L, D)


def _outproj_kernel(x_ref, of_ref, ob_ref, sg_ref, yf_ref, gt_ref, gn_ref, w_ref, o_ref):
    o = of_ref[0].astype(F32) + ob_ref[0].astype(F32)
    parts = []
    for h in range(N_HEADS):
        cols = slice(h * HEAD, (h + 1) * HEAD)
        parts.append(_rms(o[:, cols], gn_ref[...]))
    o = jnp.concatenate(parts, axis=1) * sg_ref[0].astype(F32)
    mix = jnp.concatenate([o.astype(BF16), yf_ref[0]], axis=1)
    y = jnp.dot(mix, w_ref[...], preferred_element_type=F32)
    o_ref[0] = x_ref[0] + gt_ref[0] * y


def _outproj(x, of, ob, sg, yf, gt1, gn, w_out, tm):
    B, L, D = x.shape
    tm = min(tm, L)
    tok = lambda b, i: (b, i, 0)
    half = pl.BlockSpec((1, tm, D_HGRN), tok)
    return pl.pallas_call(
        _outproj_kernel,
        grid=(B, L // tm),
        in_specs=[pl.BlockSpec((1, tm, D), tok), half, half, half, half,
                  pl.BlockSpec((1, 1, D), lambda b, i: (b, 0, 0)),
                  pl.BlockSpec((1, HEAD), lambda b, i: (0, 0)),
                  pl.BlockSpec(w_out.shape, lambda b, i: (0, 0))],
        out_specs=pl.BlockSpec((1, tm, D), tok),
        out_shape=jax.ShapeDtypeStruct((B, L, D), F32),
        compiler_params=_params("arbitrary", "arbitrary"),
        name="outproj",
    )(x, of, ob, sg, yf, gt1, gn, w_out)


_GELU_C0 = 0.7978845608028654
_GELU_C1 = 0.044715
_LOG2E = 1.4426950408889634


def _gelu_gate(a, u):
    k0 = -2.0 * _GELU_C0 * _LOG2E
    k1 = k0 * _GELU_C1
    return (a * u) / (1.0 + jnp.exp2(a * (k0 + k1 * (a * a))))


def _ffn_kernel(tm, nt, tf, x_ref, top_ref, bot_ref, sh_ref, sc_ref, gt_ref, g2_ref, gf_ref,
                wa_ref, wu_ref, wdw_ref, bdw_ref, wd_ref, o_ref, h_ref):
    i = pl.program_id(1)
    W = GRID_W

    def prep(xf):
        return _rms(xf, g2_ref[...]) * (1.0 + sc_ref[0]) + sh_ref[0]

    h_ref[0:W, :] = jnp.where(i == 0, 0.0, prep(top_ref[0])).astype(BF16)
    h_ref[W:W + tm, :] = prep(x_ref[0]).astype(BF16)
    h_ref[W + tm:, :] = jnp.where(i == nt - 1, 0.0, prep(bot_ref[0])).astype(BF16)

    colx = lax.broadcasted_iota(jnp.int32, (W, 1), 0)
    first_col = colx == 0
    last_col = colx == W - 1
    nf = wd_ref.shape[0] // tf

    def gate_tile(f):
        cols = slice(f * tf, (f + 1) * tf)
        a = jnp.dot(h_ref[...], wa_ref[:, cols], preferred_element_type=F32)
        u = jnp.dot(h_ref[W:W + tm, :], wu_ref[:, cols], preferred_element_type=F32)
        pieces = {}
        for l in range(0, tf, 128):
            lanes = slice(l, l + 128)
            wl = wdw_ref[:, f * tf + l:f * tf + l + 128]
            bias = bdw_ref[:, f * tf + l:f * tf + l + 128]
            w_left = [jnp.where(last_col, 0.0, wl[3 * dr:3 * dr + 1]) for dr in range(3)]
            w_mid = [wl[3 * dr + 1:3 * dr + 2] for dr in range(3)]
            w_right = [jnp.where(first_col, 0.0, wl[3 * dr + 2:3 * dr + 3]) for dr in range(3)]
            for r in range(0, tm, W):
                t0, t1, t2 = (a[r + dr * W:r + (dr + 1) * W, lanes] for dr in range(3))
                c_left, c_mid, c_right = (w[0] * t0 + w[1] * t1 + w[2] * t2 for w in (w_left, w_mid, w_right))
                conv = pltpu.roll(c_left, 1, axis=0) + (c_mid + bias) + pltpu.roll(c_right, W - 1, axis=0)
                pieces[r, l] = _gelu_gate(conv, u[r:r + W, lanes]).astype(BF16)
        row_blocks = [jnp.concatenate([pieces[r, l] for l in range(0, tf, 128)], axis=1)
                      for r in range(0, tm, W)]
        return jnp.concatenate(row_blocks, axis=0)

    acc = None
    for f0 in range(0, nf, 2):
        tiles = [gate_tile(f) for f in range(f0, min(f0 + 2, nf))]
        g = tiles[0] if len(tiles) == 1 else jnp.concatenate(tiles, axis=1)
        d = jnp.dot(g, wd_ref[f0 * tf:(f0 + len(tiles)) * tf, :], preferred_element_type=F32)
        acc = d if acc is None else acc + d
    x2 = x_ref[0] + gt_ref[0] * acc
    o_ref[0] = _rms(x2, gf_ref[...])


def _ffn(x, sh, sc, gt, g2, gf, w_up_a, w_up_u, w_dw, b_dw, w_down, tm, tf):
    B, L, D = x.shape
    d_ff = w_down.shape[0]
    tm = min(tm, L)
    nt = L // tm
    rpt = tm // GRID_W
    n_rows = L // GRID_W
    tok = lambda b, i: (b, i, 0)
    vec = lambda b, i: (b, 0, 0)
    nf = d_ff // tf
    resident = lambda a: pl.BlockSpec(a.shape, lambda b, i: (0, 0), pipeline_mode=pl.Buffered(1))
    return pl.pallas_call(
        functools.partial(_ffn_kernel, tm, nt, tf),
        grid=(B, nt),
        in_specs=[pl.BlockSpec((1, tm, D), tok),
                  pl.BlockSpec((1, GRID_W, D), lambda b, i: (b, jnp.maximum(i * rpt - 1, 0), 0)),
                  pl.BlockSpec((1, GRID_W, D), lambda b, i: (b, jnp.minimum((i + 1) * rpt, n_rows - 1), 0)),
                  pl.BlockSpec((1, 1, D), vec), pl.BlockSpec((1, 1, D), vec), pl.BlockSpec((1, 1, D), vec),
                  pl.BlockSpec((1, D), lambda b, i: (0, 0)),
                  pl.BlockSpec((1, D), lambda b, i: (0, 0)),
                  resident(w_up_a), resident(w_up_u), resident(w_dw), resident(b_dw), resident(w_down)],
        out_specs=pl.BlockSpec((1, tm, D), tok),
        out_shape=jax.ShapeDtypeStruct((B, L, D), F32),
        scratch_shapes=[pltpu.VMEM((tm + 2 * GRID_W, D), BF16)],
        compiler_params=_params("arbitrary", "arbitrary"),
        name="ffn",
    )(x, x, x, sh, sc, gt, g2, gf, w_up_a, w_up_u, w_dw, b_dw, w_down)


def kernel(x, c, ctx, c_ctx, w_mod, b_mod, norm1, norm2, w_in, lb_fwd, lb_bwd, hgrn_norm,
           w_out, w_up, w_dw, b_dw, w_down, norm_f):
    B, L, D = x.shape
    depth = w_mod.shape[0]
    assert depth == 1, "single-layer block"
    l = 0
    d_ff = w_down.shape[1]

    n_mod = -(-(B + 1) // 8) * 8
    cc = jnp.zeros((n_mod, D), F32).at[:B].set(c).at[B].set(c_ctx)
    mod = _modulation(cc, w_mod[l], b_mod[l][None, :])
    sh1, sc1, gt1, sh2, sc2, gt2 = [mod[:, j * D:(j + 1) * D][:, None, :] for j in range(6)]
    lat = lambda m: m[:B]
    ctxrow = lambda m: jnp.broadcast_to(m[B:B + 1], (B, 1, D))

    lb = _lower_bounds(lb_fwd, lb_bwd, l)
    w_in_b = w_in[l].astype(BF16)
    g1 = norm1[l][None, :]

    kf_c, lff_c, kb_c, lfb_c, v_c = _inproj(ctx, ctxrow(sh1), ctxrow(sc1), g1, lb, w_in_b,
                                             (_G_FF, _G_FB, _G_V), tm=256)
    zero = jnp.zeros((B, N_HEADS, HEAD, HEAD), F32)
    s0f, s0b = _gla(None, kf_c, lff_c, kb_c, lfb_c, v_c, zero, zero, need_o=False, tl=256)

    q, kf, lff, kb, lfb, v, sg, u = _inproj(x, lat(sh1), lat(sc1), g1, lb, w_in_b,
                                            (_G_Q, _G_FF, _G_FB, _G_V, _G_G, _G_U), tm=512)
    of, ob, _, _ = _gla(q, kf, lff, kb, lfb, v, s0f, s0b, need_o=True, tl=256)
    yf = _fourier(u)
    x1 = _outproj(x, of, ob, sg, yf, lat(gt1), hgrn_norm[l][None, :], w_out[l].astype(BF16), tm=512)

    w_up_b = w_up[l].astype(BF16)
    return _ffn(x1, lat(sh2), lat(sc2), lat(gt2), norm2[l][None, :], norm_f[None, :],
                w_up_b[:, :d_ff], w_up_b[:, d_ff:], w_dw[l].reshape(9, d_ff), b_dw[l][None, :],
                w_down[l].astype(BF16), tm=1024, tf=256)
```
